```python
import jax
import jax.numpy as jnp
from jax import lax
import numpy as np

D_MODEL = 1024
BATCH = 2
SEQ = 8192
DEPTH = 1

DN_HEADS = 4
DN_HEAD_DIM = 128
DN_WIDTH = DN_HEADS * DN_HEAD_DIM
CONV_WIDTH = 4
DN_CHUNK = 64
SG_GROUPS = 4
SG_GROUP_DIM = 128
SG_WIDTH = SG_GROUPS * SG_GROUP_DIM
SG_CHUNK = 128
N_EXPERTS = 32
TOP_K = 4
D_FF = 1024
SWIGLU_LIMIT = 7.0
SWIGLU_ALPHA = 1.702
MOE_BLOCK = 128
RMS_EPS = 1e-6
LN_EPS = 1e-5
L2_EPS = 1e-6

OFF_QKV = 0
OFF_Z = OFF_QKV + 3 * DN_WIDTH
OFF_BETA = OFF_Z + DN_WIDTH
OFF_A = OFF_BETA + DN_HEADS
OFF_U = OFF_A + DN_HEADS
OFF_V = OFF_U + SG_WIDTH
OFF_GA = OFF_V + SG_WIDTH
OFF_GB = OFF_GA + D_MODEL
IN_COLS = OFF_GB + D_MODEL

kernel_name = 'hybrid_gdn_sgmlp_moe'


def rmsnorm(x, g):
    xf = x.astype(jnp.float32)
    y = xf * lax.rsqrt(jnp.mean(xf * xf, axis=-1, keepdims=True) + RMS_EPS)
    return (y * g.astype(jnp.float32)).astype(x.dtype)


def layernorm(x, g, b):
    xf = x.astype(jnp.float32)
    mu = jnp.mean(xf, axis=-1, keepdims=True)
    var = jnp.mean(jnp.square(xf - mu), axis=-1, keepdims=True)
    y = (xf - mu) * lax.rsqrt(var + LN_EPS)
    return (y * g.astype(jnp.float32) + b.astype(jnp.float32)).astype(x.dtype)


def l2norm(t):
    return t * lax.rsqrt(jnp.sum(t * t, axis=-1, keepdims=True) + L2_EPS)


def causal_depthwise_conv(x, w):
    c = x.shape[-1]
    return lax.conv_general_dilated(
        x, w.reshape(CONV_WIDTH, 1, c).astype(x.dtype), window_strides=(1,),
        padding=[(CONV_WIDTH - 1, 0)], dimension_numbers=('NWC', 'WIO', 'NWC'),
        feature_group_count=c)


def gated_delta_rule(q, k, v, beta, g):
    f32 = jnp.float32
    bsz, nh, s, dk = q.shape
    dv = v.shape[-1]
    c = DN_CHUNK
    nc = s // c
    q = (l2norm(q.astype(f32)) * (dk ** -0.5)).reshape(bsz, nh, nc, c, dk)
    k = l2norm(k.astype(f32)).reshape(bsz, nh, nc, c, dk)
    v = v.astype(f32).reshape(bsz, nh, nc, c, dv)
    beta = beta.astype(f32).reshape(bsz, nh, nc, c)
    gam = jnp.cumsum(g.astype(f32).reshape(bsz, nh, nc, c), axis=-1)
    incl = jnp.tril(jnp.ones((c, c), dtype=bool))
    strict = jnp.tril(jnp.ones((c, c), dtype=bool), k=-1)
    decay = jnp.exp(jnp.where(incl, gam[..., :, None] - gam[..., None, :], -jnp.inf))
    k_beta = k * beta[..., None]
    a = jnp.where(strict, jnp.einsum('bhnrd,bhnsd->bhnrs', k_beta, k) * decay, 0.0)
    eye = jnp.eye(c, dtype=f32)
    t_inv = lax.linalg.triangular_solve(eye + a, jnp.broadcast_to(eye, a.shape),
                                        left_side=True, lower=True, unit_diagonal=True)
    u = jnp.einsum('bhnrs,bhnsv->bhnrv', t_inv, v * beta[..., None])
    w = jnp.einsum('bhnrs,bhnsd->bhnrd', t_inv, k_beta * jnp.exp(gam)[..., None])
    qk = jnp.einsum('bhnrd,bhnsd->bhnrs', q, k) * decay
    g_last = gam[..., -1]
    q_dec = q * jnp.exp(gam)[..., None]
    k_dec = k * jnp.exp(g_last[..., None] - gam)[..., None]

    def step(state, xs):
        q_c, k_c, u_c, w_c, qk_c, gl_c = xs
        v_new = u_c - jnp.einsum('bhrd,bhdv->bhrv', w_c, state)
        o_c = jnp.einsum('bhrd,bhdv->bhrv', q_c, state) + jnp.einsum('bhrs,bhsv->bhrv', qk_c, v_new)
        state = state * jnp.exp(gl_c)[..., None, None] + jnp.einsum('bhrd,bhrv->bhdv', k_c, v_new)
        return state, o_c

    xs = tuple(jnp.moveaxis(t, 2, 0) for t in (q_dec, k_dec, u, w, qk, g_last))
    state0 = jnp.zeros((bsz, nh, dk, dv), f32)
    _, o = lax.scan(step, state0, xs)
    return jnp.moveaxis(o, 0, 2).reshape(bsz, nh, s, dv)


def routed_moe(h, w_router, b_router, w_gate_up, b_gate_up, w_down, b_down):
    f32 = jnp.float32
    n_tok = h.shape[0]
    logits = h.astype(f32) @ w_router.astype(f32) + b_router.astype(f32)
    top_logits, top_idx = lax.top_k(logits, TOP_K)
    top_w = jax.nn.softmax(top_logits, axis=-1)
    n_slots = n_tok * TOP_K
    e_flat = top_idx.reshape(n_slots)
    tok_flat = jnp.repeat(jnp.arange(n_tok, dtype=jnp.int32), TOP_K)
    w_flat = top_w.reshape(n_slots)
    order = jnp.argsort(e_flat)
    e_sorted = e_flat[order]
    counts = jnp.bincount(e_flat, length=N_EXPERTS)
    padded = (counts + MOE_BLOCK - 1) // MOE_BLOCK * MOE_BLOCK
    start = jnp.cumsum(counts) - counts
    pad_end = jnp.cumsum(padded)
    pad_start = pad_end - padded
    dest = pad_start[e_sorted] + (jnp.arange(n_slots, dtype=jnp.int32) - start[e_sorted])
    n_blocks = -(-n_slots // MOE_BLOCK) + N_EXPERTS
    n_pad = n_blocks * MOE_BLOCK
    buf_tok = jnp.zeros((n_pad,), jnp.int32).at[dest].set(tok_flat[order])
    buf_w = jnp.zeros((n_pad,), f32).at[dest].set(w_flat[order])
    block_expert = jnp.minimum(
        jnp.searchsorted(pad_end, jnp.arange(n_blocks, dtype=jnp.int32) * MOE_BLOCK, side='right'),
        N_EXPERTS - 1)

    def expert_block(args):
        tok, e = args
        xb = h[tok]
        gu = xb @ w_gate_up[e] + b_gate_up[e]
        x_glu = jnp.minimum(gu[:, :D_FF], SWIGLU_LIMIT)
        x_lin = jnp.clip(gu[:, D_FF:], -SWIGLU_LIMIT, SWIGLU_LIMIT)
        act = x_glu * jax.nn.sigmoid(SWIGLU_ALPHA * x_glu) * (x_lin + 1.0)
        return act @ w_down[e] + b_down[e]

    ys = lax.map(expert_block, (buf_tok.reshape(n_blocks, MOE_BLOCK), block_expert))
    ys = ys.reshape(n_pad, -1) * buf_w[:, None].astype(ys.dtype)
    return jax.ops.segment_sum(ys, buf_tok, num_segments=n_tok)


def hybrid_layer(x, norm1_g, w_in, conv_w, dn_a_log, dn_dt_bias, dn_norm_g, w_o_dn,
                 sg_ln_g, sg_ln_b, sg_w, sg_b, w_o_sg, w_out, norm2_g, w_router, b_router,
                 w_gate_up, b_gate_up, w_down, b_down):
    bsz, s, d = x.shape
    f32 = jnp.float32
    h = rmsnorm(x, norm1_g)
    proj = h @ w_in

    qkv = jax.nn.silu(causal_depthwise_conv(proj[..., OFF_QKV:OFF_Z], conv_w))
    q, k, v = jnp.split(qkv, 3, axis=-1)
    to_heads = lambda t: t.reshape(bsz, s, DN_HEADS, DN_HEAD_DIM).transpose(0, 2, 1, 3)
    beta = jax.nn.sigmoid(proj[..., OFF_BETA:OFF_A].astype(f32)).transpose(0, 2, 1)
    g = -jnp.exp(dn_a_log.astype(f32)) * jax.nn.softplus(
        proj[..., OFF_A:OFF_U].astype(f32) + dn_dt_bias.astype(f32))
    o = gated_delta_rule(to_heads(q), to_heads(k), to_heads(v), beta, g.transpose(0, 2, 1))
    o = o.astype(x.dtype).transpose(0, 2, 1, 3)
    z = proj[..., OFF_Z:OFF_BETA].reshape(bsz, s, DN_HEADS, DN_HEAD_DIM)
    y_a = (rmsnorm(o, dn_norm_g) * jax.nn.silu(z)).reshape(bsz, s, DN_WIDTH)

    uv = jax.nn.gelu(proj[..., OFF_U:OFF_GA], approximate=False)
    u, v_sg = uv[..., :SG_WIDTH], uv[..., SG_WIDTH:]
    v_sg = layernorm(v_sg, sg_ln_g, sg_ln_b).reshape(bsz, s // SG_CHUNK, SG_CHUNK, SG_GROUPS, SG_GROUP_DIM)
    w_s = jnp.where(jnp.tril(jnp.ones((SG_CHUNK, SG_CHUNK), dtype=bool)), sg_w, 0.0)
    v_mix = jnp.einsum('gts,bnsgc->bntgc', w_s, v_sg) + sg_b.T[None, None, :, :, None]
    y_b = u * v_mix.reshape(bsz, s, SG_WIDTH)

    gate_a = jax.nn.sigmoid(proj[..., OFF_GA:OFF_GB])
    gate_b = jax.nn.sigmoid(proj[..., OFF_GB:IN_COLS])
    merged = gate_a * (y_a @ w_o_dn) + gate_b * (y_b @ w_o_sg)
    x = x + merged @ w_out

    h2 = rmsnorm(x, norm2_g).reshape(bsz * s, d)
    y_moe = routed_moe(h2, w_router, b_router, w_gate_up, b_gate_up, w_down, b_down)
    return x + y_moe.reshape(bsz, s, d)


def setup_inputs(seed: int = 0) -> dict:
    key = jax.random.key(seed)
    ks = jax.random.split(key, 22)
    nrm = lambda k, shape, fan_in: jax.random.normal(k, shape, jnp.float32) * (fan_in ** -0.5)
    L = DEPTH
    return {
        'x': jax.random.normal(ks[0], (BATCH, SEQ, D_MODEL), jnp.float32),
        'norm1_g': 1.0 + 0.02 * jax.random.normal(ks[1], (L, D_MODEL), jnp.float32),
        'w_in': nrm(ks[2], (L, D_MODEL, IN_COLS), D_MODEL),
        'conv_w': nrm(ks[3], (L, CONV_WIDTH, 3 * DN_WIDTH), CONV_WIDTH),
        'dn_a_log': jnp.log(jax.random.uniform(ks[4], (L, DN_HEADS), jnp.float32, 1.0, 16.0)),
        'dn_dt_bias': 0.1 * jax.random.normal(ks[5], (L, DN_HEADS), jnp.float32),
        'dn_norm_g': 1.0 + 0.02 * jax.random.normal(ks[6], (L, DN_HEAD_DIM), jnp.float32),
        'w_o_dn': nrm(ks[7], (L, DN_WIDTH, D_MODEL), DN_WIDTH),
        'sg_ln_g': 1.0 + 0.02 * jax.random.normal(ks[8], (L, SG_WIDTH), jnp.float32),
        'sg_ln_b': 0.02 * jax.random.normal(ks[9], (L, SG_WIDTH), jnp.float32),
        'sg_w': nrm(ks[10], (L, SG_GROUPS, SG_CHUNK, SG_CHUNK), SG_CHUNK),
        'sg_b': 1.0 + 0.02 * jax.random.normal(ks[11], (L, SG_GROUPS, SG_CHUNK), jnp.float32),
        'w_o_sg': nrm(ks[12], (L, SG_WIDTH, D_MODEL), SG_WIDTH),
        'w_out': nrm(ks[13], (L, D_MODEL, D_MODEL), D_MODEL),
        'norm2_g': 1.0 + 0.02 * jax.random.normal(ks[14], (L, D_MODEL), jnp.float32),
        'w_router': nrm(ks[15], (L, D_MODEL, N_EXPERTS), D_MODEL),
        'b_router': 0.01 * jax.random.normal(ks[16], (L, N_EXPERTS), jnp.float32),
        'w_gate_up': nrm(ks[17], (L, N_EXPERTS, D_MODEL, 2 * D_FF), D_MODEL),
        'b_gate_up': 0.01 * jax.random.normal(ks[18], (L, N_EXPERTS, 2 * D_FF), jnp.float32),
        'w_down': nrm(ks[19], (L, N_EXPERTS, D_FF, D_MODEL), D_FF),
        'b_down': 0.01 * jax.random.normal(ks[20], (L, N_EXPERTS, D_MODEL), jnp.float32),
        'norm_f_g': 1.0 + 0.02 * jax.random.normal(ks[21], (D_MODEL,), jnp.float32),
    }


def reference(x, norm1_g, w_in, conv_w, dn_a_log, dn_dt_bias, dn_norm_g, w_o_dn,
              sg_ln_g, sg_ln_b, sg_w, sg_b, w_o_sg, w_out, norm2_g, w_router, b_router,
              w_gate_up, b_gate_up, w_down, b_down, norm_f_g):
    for layer in range(DEPTH):
        x = hybrid_layer(
            x, norm1_g[layer], w_in[layer], conv_w[layer], dn_a_log[layer], dn_dt_bias[layer],
            dn_norm_g[layer], w_o_dn[layer], sg_ln_g[layer], sg_ln_b[layer], sg_w[layer],
            sg_b[layer], w_o_sg[layer], w_out[layer], norm2_g[layer], w_router[layer],
            b_router[layer], w_gate_up[layer], b_gate_up[layer], w_down[layer], b_down[layer])
    return rmsnorm(x, norm_f_g)
```

```python
import functools

import jax
import jax.numpy as jnp
from jax import lax
from jax.experimental import pallas as pl
from jax.experimental.pallas import tpu as pltpu
from jax.experimental.pallas import tpu_sc as plsc

F32 = jnp.float32
BF16 = jnp.bfloat16

DN_HEADS = 4
DN_HEAD_DIM = 128
DN_WIDTH = DN_HEADS * DN_HEAD_DIM
CONV_WIDTH = 4
SG_GROUPS = 4
SG_GROUP_DIM = 128
SG_WIDTH = SG_GROUPS * SG_GROUP_DIM
SG_CHUNK = 128
N_EXPERTS = 32
TOP_K = 4
SWIGLU_LIMIT = 7.0
SWIGLU_ALPHA = 1.702
RMS_EPS = 1e-6
LN_EPS = 1e-5
L2_EPS = 1e-6

OFF_Z = 3 * DN_WIDTH
OFF_BETA = OFF_Z + DN_WIDTH
OFF_U = OFF_BETA + 2 * DN_HEADS

LANES = 128
SUBLANES = 8
SC_CORES = 2
SC_SUBCORES = 16
SC_WORKERS = SC_CORES * SC_SUBCORES
VMEM_LIMIT = 56 * 1024 * 1024

ROW_TILE = 512
GDN_TILE = 512
GDN_CHUNK = 128
MOE_TILE = 512
SC_CHUNK = 64

_NT = (((1,), (1,)), ((), ()))
_TN = (((0,), (0,)), ((), ()))


def _dot(a, b):
    return jnp.dot(a, b, preferred_element_type=F32)


def _dot_nt(a, b):
    return lax.dot_general(a, b, _NT, preferred_element_type=F32)


def _dot_tn(a, b):
    return lax.dot_general(a, b, _TN, preferred_element_type=F32)


def _sigmoid(x):
    return 1.0 / (1.0 + jnp.exp(-x))


def _softplus(x):
    return jnp.maximum(x, 0.0) + jnp.log1p(jnp.exp(-jnp.abs(x)))


def _split3(x):
    hi = x.astype(BF16)
    r = x - hi.astype(F32)
    mid = r.astype(BF16)
    lo = (r - mid.astype(F32)).astype(BF16)
    return hi, mid, lo


def _const_spec(shape):
    nd = len(shape)
    return pl.BlockSpec(shape, lambda *_: (0,) * nd)


def _params(sem):
    return pltpu.CompilerParams(dimension_semantics=sem, vmem_limit_bytes=VMEM_LIMIT)


def _proj_kernel(x_ref, g1_ref, wqkv_ref, wz_ref, wba_ref, wbat_ref, wuv_ref, wga_ref, wgb_ref,
                 alog_r_ref, dtb_r_ref, alog_c_ref, dtb_c_ref, lng_ref, lnb_ref,
                 qkv_ref, zs_ref, bg_ref, bgt_ref, u_ref, v_ref, ga_ref, gb_ref):
    x = x_ref[...]
    ms = jnp.mean(x * x, axis=-1, keepdims=True)
    h = (x * lax.rsqrt(ms + RMS_EPS) * g1_ref[...]).astype(BF16)

    qkv_ref[...] = _dot(h, wqkv_ref[...])

    z = _dot(h, wz_ref[...])
    zs_ref[...] = (z * _sigmoid(z)).astype(BF16)

    ba = _dot(h, wba_ref[...])
    lane = lax.broadcasted_iota(jnp.int32, ba.shape, 1)
    g_col = -jnp.exp(alog_r_ref[...]) * _softplus(ba + dtb_r_ref[...])
    bg_ref[...] = jnp.where(lane < DN_HEADS, _sigmoid(ba), g_col)
    bat = _dot_nt(wbat_ref[...], h)
    row = lax.broadcasted_iota(jnp.int32, bat.shape, 0)
    g_row = -jnp.exp(alog_c_ref[...]) * _softplus(bat + dtb_c_ref[...])
    bgt_ref[...] = jnp.where(row < DN_HEADS, _sigmoid(bat), g_row)

    uv = _dot(h, wuv_ref[...])
    ge = 0.5 * uv * (1.0 + lax.erf(uv * (0.5 ** 0.5)))
    u_ref[...] = ge[:, :SG_WIDTH].astype(BF16)
    v = ge[:, SG_WIDTH:]
    mu = jnp.mean(v, axis=-1, keepdims=True)
    vc = v - mu
    var = jnp.mean(vc * vc, axis=-1, keepdims=True)
    v_ref[...] = (vc * lax.rsqrt(var + LN_EPS) * lng_ref[...] + lnb_ref[...]).astype(BF16)

    ga_ref[...] = _sigmoid(_dot(h, wga_ref[...])).astype(BF16)
    gb_ref[...] = _sigmoid(_dot(h, wgb_ref[...])).astype(BF16)


def _project(x2, norm1_g, w_in, dn_a_log, dn_dt_bias, sg_ln_g, sg_ln_b):
    t, d = x2.shape
    tm = ROW_TILE
    wb = w_in.astype(BF16)
    wqkv = wb[:, :OFF_Z]
    wz = wb[:, OFF_Z:OFF_BETA]
    wba8 = wb[:, OFF_BETA:OFF_U]
    wba = jnp.pad(wba8, ((0, 0), (0, LANES - 2 * DN_HEADS)))
    wbat = wba8.T
    wuv = wb[:, OFF_U:OFF_U + 2 * SG_WIDTH]
    wga = wb[:, OFF_U + 2 * SG_WIDTH:OFF_U + 2 * SG_WIDTH + d]
    wgb = wb[:, OFF_U + 2 * SG_WIDTH + d:]
    zeros4 = jnp.zeros((DN_HEADS,), F32)
    alog8 = jnp.concatenate([zeros4, dn_a_log.astype(F32)])
    dtb8 = jnp.concatenate([zeros4, dn_dt_bias.astype(F32)])
    alog_r = jnp.pad(alog8, (0, LANES - 2 * DN_HEADS)).reshape(1, LANES)
    dtb_r = jnp.pad(dtb8, (0, LANES - 2 * DN_HEADS)).reshape(1, LANES)
    alog_c = alog8.reshape(2 * DN_HEADS, 1)
    dtb_c = dtb8.reshape(2 * DN_HEADS, 1)

    row = lambda w: pl.BlockSpec((tm, w), lambda i: (i, 0))
    out_shape = (
        jax.ShapeDtypeStruct((t, OFF_Z), F32),
        jax.ShapeDtypeStruct((t, DN_WIDTH), BF16),
        jax.ShapeDtypeStruct((t, LANES), F32),
        jax.ShapeDtypeStruct((2 * DN_HEADS, t), F32),
        jax.ShapeDtypeStruct((t, SG_WIDTH), BF16),
        jax.ShapeDtypeStruct((t, SG_WIDTH), BF16),
        jax.ShapeDtypeStruct((t, d), BF16),
        jax.ShapeDtypeStruct((t, d), BF16),
    )
    return pl.pallas_call(
        _proj_kernel,
        out_shape=out_shape,
        grid=(t // tm,),
        in_specs=[
            row(d), _const_spec((1, d)),
            _const_spec(wqkv.shape), _const_spec(wz.shape), _const_spec(wba.shape),
            _const_spec(wbat.shape), _const_spec(wuv.shape), _const_spec(wga.shape),
            _const_spec(wgb.shape),
            _const_spec((1, LANES)), _const_spec((1, LANES)),
            _const_spec((2 * DN_HEADS, 1)), _const_spec((2 * DN_HEADS, 1)),
            _const_spec((1, SG_WIDTH)), _const_spec((1, SG_WIDTH)),
        ],
        out_specs=(
            row(OFF_Z), row(DN_WIDTH), row(LANES),
            pl.BlockSpec((2 * DN_HEADS, tm), lambda i: (0, i)),
            row(SG_WIDTH), row(SG_WIDTH), row(d), row(d),
        ),
        compiler_params=_params(("parallel",)),
        name="in_proj",
    )(x2, norm1_g.reshape(1, d), wqkv, wz, wba, wbat, wuv, wga, wgb,
      alog_r, dtb_r, alog_c, dtb_c, sg_ln_g.reshape(1, SG_WIDTH), sg_ln_b.reshape(1, SG_WIDTH))


_M_DIAG8, _M_OFF16, _M_OFF32, _M_OFF64, _M_OFF128, _M_INCL, _M_STRICT, _M_EYE, _M_UPPER = range(9)


def _gdn_constants(c):
    r = jnp.arange(c)[:, None]
    s = jnp.arange(c)[None, :]
    same = lambda b: (r // b) == (s // b)
    mats = [same(8)]
    b = 8
    while b < c:
        mats.append(same(2 * b) & ~same(b) & (r > s))
        b *= 2
    mats += [r >= s, r > s, r == s, r <= s]
    return jnp.stack(mats).astype(F32)


def _unit_lower_inverse(a, cm_ref):
    a0 = (a * cm_ref[_M_DIAG8]).astype(BF16)
    a2 = _dot(a0, a0)
    t = cm_ref[_M_EYE] - a0.astype(F32)
    a2b = a2.astype(BF16)
    t = t + _dot(t.astype(BF16), a2b)
    a4 = _dot(a2b, a2b)
    t = t + _dot(t.astype(BF16), a4.astype(BF16))
    n_levels = cm_ref.shape[0] - 5
    for lvl in range(n_levels):
        off = (a * cm_ref[_M_OFF16 + lvl]).astype(BF16)
        tb = t.astype(BF16)
        t = t - _dot(_dot(tb, off).astype(BF16), tb)
    return t


def _gdn_kernel(qkv_ref, bg_ref, bgt_ref, zs_ref, cw_ref, gn_ref, cm_ref, ya_ref,
                xbuf, s_ref, q_s, k_s, v_s):
    c = pl.program_id(1)
    lt = qkv_ref.shape[0]
    ch = cm_ref.shape[1]
    hd = DN_HEAD_DIM
    halo = SUBLANES

    @pl.when(c == 0)
    def _():
        s_ref[...] = jnp.zeros_like(s_ref)
        xbuf[0:halo, :] = jnp.zeros((halo, xbuf.shape[1]), F32)

    @pl.when(c > 0)
    def _():
        xbuf[0:halo, :] = xbuf[lt:lt + halo, :]

    xbuf[halo:lt + halo, :] = qkv_ref[...]
    cw = cw_ref[...]
    conv = cw[CONV_WIDTH - 1:CONV_WIDTH, :] * xbuf[halo:lt + halo, :]
    for kk in range(CONV_WIDTH - 1):
        conv = conv + cw[kk:kk + 1, :] * xbuf[pl.ds(halo - (CONV_WIDTH - 1) + kk, lt), :]
    act = conv * _sigmoid(conv)
    for h in range(DN_HEADS):
        qh = act[:, h * hd:(h + 1) * hd]
        kh = act[:, DN_WIDTH + h * hd:DN_WIDTH + (h + 1) * hd]
        q_s[h] = qh * (lax.rsqrt(jnp.sum(qh * qh, axis=-1, keepdims=True) + L2_EPS) * (hd ** -0.5))
        k_s[h] = kh * lax.rsqrt(jnp.sum(kh * kh, axis=-1, keepdims=True) + L2_EPS)
        v_s[h] = act[:, 2 * DN_WIDTH + h * hd:2 * DN_WIDTH + (h + 1) * hd]

    gn = gn_ref[...]
    lower16 = cm_ref[_M_INCL].astype(BF16)
    upper16 = cm_ref[_M_UPPER].astype(BF16)

    def chunk_body(ci, carry):
        r0 = pl.multiple_of(ci * ch, ch)
        bgc = bg_ref[pl.ds(r0, ch), :]
        hi, mid, lo = _split3(bgc)
        gam = _dot(lower16, hi) + _dot(lower16, mid) + _dot(lower16, lo)
        gam_last = gam[ch - 1:ch, :]
        e_gam = jnp.exp(gam)
        e_rest = jnp.exp(gam_last - gam)
        e_last = jnp.exp(gam_last)
        bgt = bgt_ref[:, pl.ds(r0, ch)]
        hi, mid, lo = _split3(bgt)
        gam_t = _dot(hi, upper16) + _dot(mid, upper16) + _dot(lo, upper16)

        for h in range(DN_HEADS):
            beta = bgc[:, h:h + 1]
            gcol = gam[:, DN_HEADS + h:DN_HEADS + h + 1]
            grow = gam_t[DN_HEADS + h:DN_HEADS + h + 1, :]
            decay = jnp.exp(jnp.where(cm_ref[_M_INCL] > 0.0, gcol - grow, -jnp.inf))
            q = q_s[h, pl.ds(r0, ch), :]
            k = k_s[h, pl.ds(r0, ch), :]
            v = v_s[h, pl.ds(r0, ch), :]
            kb = k * beta
            k16 = k.astype(BF16)
            kb16 = kb.astype(BF16)
            a = _dot_nt(kb16, k16) * decay * cm_ref[_M_STRICT]
            qk = _dot_nt(q.astype(BF16), k16) * decay
            t_inv = _unit_lower_inverse(a, cm_ref).astype(BF16)
            eg = e_gam[:, DN_HEADS + h:DN_HEADS + h + 1]
            u = _dot(t_inv, (v * beta).astype(BF16))
            w = _dot(t_inv, (kb * eg).astype(BF16))
            state = s_ref[h]
            st16 = state.astype(BF16)
            v_new = u - _dot(w.astype(BF16), st16)
            vn16 = v_new.astype(BF16)
            o = _dot((q * eg).astype(BF16), st16) + _dot(qk.astype(BF16), vn16)
            k_dec = k * e_rest[:, DN_HEADS + h:DN_HEADS + h + 1]
            s_ref[h] = state * e_last[:, DN_HEADS + h:DN_HEADS + h + 1] + _dot_tn(k_dec.astype(BF16), vn16)
            on = o * lax.rsqrt(jnp.mean(o * o, axis=-1, keepdims=True) + RMS_EPS) * gn
            zs = zs_ref[pl.ds(r0, ch), h * hd:(h + 1) * hd].astype(F32)
            ya_ref[pl.ds(r0, ch), h * hd:(h + 1) * hd] = (on * zs).astype(BF16)
        return carry

    lax.fori_loop(0, lt // ch, chunk_body, 0)


def _gated_delta(qkv, bg, bgt, zs, conv_w, dn_norm_g, batch, seq):
    t = batch * seq
    lt = GDN_TILE
    nt = seq // lt
    cm = _gdn_constants(GDN_CHUNK)
    w3 = qkv.shape[1]
    row = lambda w: pl.BlockSpec((lt, w), lambda b, c: (b * nt + c, 0))
    return pl.pallas_call(
        _gdn_kernel,
        out_shape=jax.ShapeDtypeStruct((t, DN_WIDTH), BF16),
        grid=(batch, nt),
        in_specs=[
            row(w3), row(LANES),
            pl.BlockSpec((2 * DN_HEADS, lt), lambda b, c: (0, b * nt + c)),
            row(DN_WIDTH),
            _const_spec(conv_w.shape), _const_spec((1, DN_HEAD_DIM)), _const_spec(cm.shape),
        ],
        out_specs=row(DN_WIDTH),
        scratch_shapes=[
            pltpu.VMEM((lt + SUBLANES, w3), F32),
            pltpu.VMEM((DN_HEADS, DN_HEAD_DIM, DN_HEAD_DIM), F32),
            pltpu.VMEM((DN_HEADS, lt, DN_HEAD_DIM), F32),
            pltpu.VMEM((DN_HEADS, lt, DN_HEAD_DIM), F32),
            pltpu.VMEM((DN_HEADS, lt, DN_HEAD_DIM), F32),
        ],
        compiler_params=_params(("parallel", "arbitrary")),
        name="gated_delta",
    )(qkv, bg, bgt, zs, conv_w.astype(F32), dn_norm_g.reshape(1, DN_HEAD_DIM).astype(F32), cm)


def _merge_kernel(x_ref, ya_ref, u_ref, v_ref, ga_ref, gb_ref, sgw_ref, sgb_ref, wdn_ref, wsg_ref,
                  wout_ref, g2_ref, wrt_ref, br_ref,
                  x1_ref, h2_ref, route_ref, rw_ref, cnt_ref,
                  yb_s, carry_s):
    i = pl.program_id(0)
    tm = x_ref.shape[0]
    ne = wrt_ref.shape[0]

    @pl.when(i == 0)
    def _():
        carry_s[...] = jnp.zeros_like(carry_s)

    tri = (lax.broadcasted_iota(jnp.int32, (SG_CHUNK, SG_CHUNK), 0)
           >= lax.broadcasted_iota(jnp.int32, (SG_CHUNK, SG_CHUNK), 1))
    for g in range(SG_GROUPS):
        wg = jnp.where(tri, sgw_ref[g], 0.0).astype(BF16)
        cols = slice(g * SG_GROUP_DIM, (g + 1) * SG_GROUP_DIM)
        for n in range(tm // SG_CHUNK):
            rows = slice(n * SG_CHUNK, (n + 1) * SG_CHUNK)
            mix = _dot(wg, v_ref[rows, cols]) + sgb_ref[:, cols]
            yb_s[rows, cols] = (u_ref[rows, cols].astype(F32) * mix).astype(BF16)

    da = _dot(ya_ref[...], wdn_ref[...])
    db = _dot(yb_s[...], wsg_ref[...])
    merged = ga_ref[...].astype(F32) * da + gb_ref[...].astype(F32) * db
    x1 = x_ref[...] + _dot(merged.astype(BF16), wout_ref[...])
    x1_ref[...] = x1
    ms = jnp.mean(x1 * x1, axis=-1, keepdims=True)
    h2 = x1 * lax.rsqrt(ms + RMS_EPS) * g2_ref[...]
    h2_ref[...] = h2

    logits = _dot_nt(wrt_ref[...], h2.astype(BF16)) + br_ref[...]
    eidx = lax.broadcasted_iota(jnp.int32, (ne, tm), 0).astype(F32)
    work = logits
    onehot = jnp.zeros((ne, tm), F32)
    sels, tops, idxs = [], [], []
    for _ in range(TOP_K):
        m = jnp.max(work, axis=0, keepdims=True)
        idx = jnp.min(jnp.where(work == m, eidx, float(ne)), axis=0, keepdims=True)
        sel = eidx == idx
        work = jnp.where(sel, -jnp.inf, work)
        onehot = onehot + sel.astype(F32)
        sels.append(sel)
        tops.append(m)
        idxs.append(idx)
    ps = [jnp.exp(m - tops[0]) for m in tops]
    den = ps[0] + ps[1] + ps[2] + ps[3]
    before = (lax.broadcasted_iota(jnp.int32, (tm, tm), 0)
              < lax.broadcasted_iota(jnp.int32, (tm, tm), 1)).astype(BF16)
    cum = _dot(onehot.astype(BF16), before) + carry_s[:, 0:1]
    ranks = [jnp.sum(jnp.where(sel, cum, 0.0), axis=0, keepdims=True) for sel in sels]
    route_ref[...] = jnp.concatenate(idxs + ranks, axis=0).astype(jnp.int32)
    rw_ref[...] = jnp.concatenate([p / den for p in ps] + [jnp.zeros((TOP_K, tm), F32)], axis=0)
    carry_s[...] = carry_s[...] + jnp.sum(onehot, axis=1, keepdims=True)
    cnt_ref[...] = carry_s[...]


def _merge_route(x2, ya, u, v, ga, gb, sg_w, sg_b, w_o_dn, w_o_sg, w_out, norm2_g, w_router, b_router):
    t, d = x2.shape
    tm = ROW_TILE
    ne = w_router.shape[1]
    sgb_full = jnp.repeat(sg_b.T.astype(F32), SG_GROUP_DIM, axis=1)
    row = lambda w: pl.BlockSpec((tm, w), lambda i: (i, 0))
    col = lambda h: pl.BlockSpec((h, tm), lambda i: (0, i))
    out_shape = (
        jax.ShapeDtypeStruct((t, d), F32),
        jax.ShapeDtypeStruct((t, d), F32),
        jax.ShapeDtypeStruct((2 * TOP_K, t), jnp.int32),
        jax.ShapeDtypeStruct((2 * TOP_K, t), F32),
        jax.ShapeDtypeStruct((ne, LANES), F32),
    )
    return pl.pallas_call(
        _merge_kernel,
        out_shape=out_shape,
        grid=(t // tm,),
        in_specs=[
            row(d), row(DN_WIDTH), row(SG_WIDTH), row(SG_WIDTH), row(d), row(d),
            _const_spec(sg_w.shape), _const_spec(sgb_full.shape),
            _const_spec(w_o_dn.shape), _const_spec(w_o_sg.shape), _const_spec(w_out.shape),
            _const_spec((1, d)), _const_spec((ne, d)), _const_spec((ne, 1)),
        ],
        out_specs=(row(d), row(d), col(2 * TOP_K), col(2 * TOP_K), _const_spec((ne, LANES))),
        scratch_shapes=[pltpu.VMEM((tm, SG_WIDTH), BF16), pltpu.VMEM((ne, LANES), F32)],
        compiler_params=_params(("arbitrary",)),
        name="merge_route",
    )(x2, ya, u, v, ga, gb, sg_w.astype(F32), sgb_full, w_o_dn.astype(BF16), w_o_sg.astype(BF16),
      w_out.astype(BF16), norm2_g.reshape(1, d).astype(F32), w_router.T.astype(BF16),
      b_router.reshape(ne, 1).astype(F32))


def _sc_mesh():
    return plsc.VectorSubcoreMesh(core_axis_name="c", subcore_axis_name="s")


def _dispatch(h2, dest3, n_rows):
    n_tok, d = h2.shape
    chunk = dest3.shape[2]
    per_w = n_tok // SC_WORKERS
    n_chunks = per_w // chunk

    @functools.partial(
        pl.kernel, mesh=_sc_mesh(),
        out_type=jax.ShapeDtypeStruct((n_rows, d), h2.dtype),
        scratch_types=[pltpu.VMEM((TOP_K, chunk), jnp.int32), pltpu.VMEM((chunk, d), h2.dtype)],
        name="moe_dispatch",
    )
    def k(h2_hbm, dest_hbm, xs_hbm, idx_v, rows_v):
        wid = lax.axis_index("s") * SC_CORES + lax.axis_index("c")
        base = wid * per_w

        @pl.loop(0, n_chunks)
        def _(j):
            t0 = pl.multiple_of(base + j * chunk, chunk)
            pltpu.sync_copy(dest_hbm.at[base // chunk + j], idx_v)
            pltpu.sync_copy(h2_hbm.at[pl.ds(t0, chunk)], rows_v)
            for kk in range(TOP_K):
                pltpu.sync_copy(rows_v, xs_hbm.at[idx_v.at[kk]])

    return k(h2, dest3)


def _collect(y, dest3):
    n_tok = dest3.shape[0] * dest3.shape[2]
    chunk = dest3.shape[2]
    d = y.shape[1]
    per_w = n_tok // SC_WORKERS
    n_chunks = per_w // chunk

    @functools.partial(
        pl.kernel, mesh=_sc_mesh(),
        out_type=jax.ShapeDtypeStruct((TOP_K, n_tok, d), y.dtype),
        scratch_types=[pltpu.VMEM((TOP_K, chunk), jnp.int32), pltpu.VMEM((chunk, d), y.dtype)],
        name="moe_collect",
    )
    def k(y_hbm, dest_hbm, yg_hbm, idx_v, rows_v):
        wid = lax.axis_index("s") * SC_CORES + lax.axis_index("c")
        base = wid * per_w

        @pl.loop(0, n_chunks)
        def _(j):
            t0 = pl.multiple_of(base + j * chunk, chunk)
            pltpu.sync_copy(dest_hbm.at[base // chunk + j], idx_v)
            for kk in range(TOP_K):
                pltpu.sync_copy(y_hbm.at[idx_v.at[kk]], rows_v)
                pltpu.sync_copy(rows_v, yg_hbm.at[kk, pl.ds(t0, chunk)])

    return k(y, dest3)


def _expert_kernel(be_ref, nv_ref, xs_ref, wgu_ref, bgu_ref, wd_ref, bd_ref, y_ref):
    i = pl.program_id(0)
    dff = wd_ref.shape[1]

    @pl.when(i < nv_ref[0])
    def _():
        gu = _dot(xs_ref[...].astype(BF16), wgu_ref[0]) + bgu_ref[0]
        x_glu = jnp.minimum(gu[:, :dff], SWIGLU_LIMIT)
        x_lin = jnp.clip(gu[:, dff:], -SWIGLU_LIMIT, SWIGLU_LIMIT)
        act = x_glu * _sigmoid(SWIGLU_ALPHA * x_glu) * (x_lin + 1.0)
        y_ref[...] = _dot(act.astype(BF16), wd_ref[0]) + bd_ref[0]

    @pl.when(i >= nv_ref[0])
    def _():
        y_ref[...] = jnp.zeros_like(y_ref)


def _expert_ffn(xs, block_expert, n_valid, w_gate_up, b_gate_up, w_down, b_down):
    n_rows, d = xs.shape
    tm = MOE_TILE
    ne, _, dff2 = w_gate_up.shape
    dff = dff2 // 2
    n_blocks = n_rows // tm
    grid_spec = pltpu.PrefetchScalarGridSpec(
        num_scalar_prefetch=2,
        grid=(n_blocks,),
        in_specs=[
            pl.BlockSpec((tm, d), lambda i, be, nv: (jnp.minimum(i, nv[0] - 1), 0)),
            pl.BlockSpec((1, d, dff2), lambda i, be, nv: (be[i], 0, 0)),
            pl.BlockSpec((1, 1, dff2), lambda i, be, nv: (be[i], 0, 0)),
            pl.BlockSpec((1, dff, d), lambda i, be, nv: (be[i], 0, 0)),
            pl.BlockSpec((1, 1, d), lambda i, be, nv: (be[i], 0, 0)),
        ],
        out_specs=pl.BlockSpec((tm, d), lambda i, be, nv: (i, 0)),
    )
    return pl.pallas_call(
        _expert_kernel,
        out_shape=jax.ShapeDtypeStruct((n_rows, d), F32),
        grid_spec=grid_spec,
        compiler_params=_params(("arbitrary",)),
        name="expert_ffn",
    )(block_expert, n_valid, xs, w_gate_up.astype(BF16), b_gate_up.reshape(ne, 1, dff2).astype(F32),
      w_down.astype(BF16), b_down.reshape(ne, 1, d).astype(F32))


def _final_kernel(x1_ref, yg_ref, rw_ref, gf_ref, o_ref):
    acc = x1_ref[...]
    rw = rw_ref[...]
    for kk in range(TOP_K):
        acc = acc + rw[:, kk:kk + 1] * yg_ref[kk]
    ms = jnp.mean(acc * acc, axis=-1, keepdims=True)
    o_ref[...] = acc * lax.rsqrt(ms + RMS_EPS) * gf_ref[...]


def _combine_final(x1, yg, rw_tok, norm_f_g):
    t, d = x1.shape
    tm = ROW_TILE
    return pl.pallas_call(
        _final_kernel,
        out_shape=jax.ShapeDtypeStruct((t, d), F32),
        grid=(t // tm,),
        in_specs=[
            pl.BlockSpec((tm, d), lambda i: (i, 0)),
            pl.BlockSpec((TOP_K, tm, d), lambda i: (0, i, 0)),
            pl.BlockSpec((tm, 2 * TOP_K), lambda i: (i, 0)),
            _const_spec((1, d)),
        ],
        out_specs=pl.BlockSpec((tm, d), lambda i: (i, 0)),
        compiler_params=_params(("parallel",)),
        name="combine_final",
    )(x1, yg, rw_tok, norm_f_g.reshape(1, d).astype(F32))


def _routing_tables(route, counts, n_tok):
    tm = MOE_TILE
    n_blocks = (n_tok * TOP_K) // tm + N_EXPERTS
    cnt = counts[:, 0].astype(jnp.int32)
    padded = (cnt + tm - 1) // tm * tm
    pad_end = jnp.cumsum(padded)
    pad_start = pad_end - padded
    eid = route[:TOP_K]
    rank = route[TOP_K:]
    start = jnp.sum(jnp.where(eid[..., None] == jnp.arange(N_EXPERTS, dtype=jnp.int32),
                              pad_start, 0), axis=-1)
    dest = (start + rank).astype(jnp.int32)
    dest3 = dest.reshape(TOP_K, n_tok // SC_CHUNK, SC_CHUNK).transpose(1, 0, 2)
    n_valid = (pad_end[-1] // tm).astype(jnp.int32)
    blk = jnp.arange(n_blocks, dtype=jnp.int32)
    be = jnp.searchsorted(pad_end, jnp.minimum(blk, n_valid - 1) * tm, side="right")
    be = jnp.minimum(be, N_EXPERTS - 1).astype(jnp.int32)
    return dest3, be, n_valid.reshape(1), n_blocks * tm


def _layer(x, norm1_g, w_in, conv_w, dn_a_log, dn_dt_bias, dn_norm_g, w_o_dn, sg_ln_g, sg_ln_b,
           sg_w, sg_b, w_o_sg, w_out, norm2_g, w_router, b_router, w_gate_up, b_gate_up, w_down, b_down):
    bsz, seq, d = x.shape
    t = bsz * seq
    x2 = x.reshape(t, d)
    qkv, zs, bg, bgt, u, v, ga, gb = _project(x2, norm1_g, w_in, dn_a_log, dn_dt_bias, sg_ln_g, sg_ln_b)
    ya = _gated_delta(qkv, bg, bgt, zs, conv_w, dn_norm_g, bsz, seq)
    x1, h2, route, rw, counts = _merge_route(x2, ya, u, v, ga, gb, sg_w, sg_b, w_o_dn, w_o_sg, w_out,
                                             norm2_g, w_router, b_router)
    dest3, block_expert, n_valid, n_rows = _routing_tables(route, counts, t)
    xs = _dispatch(h2, dest3, n_rows)
    y = _expert_ffn(xs, block_expert, n_valid, w_gate_up, b_gate_up, w_down, b_down)
    yg = _collect(y, dest3)
    return x1, yg, rw.T


def kernel(x, norm1_g, w_in, conv_w, dn_a_log, dn_dt_bias, dn_norm_g, w_o_dn, sg_ln_g, sg_ln_b, sg_w, sg_b, w_o_sg, w_out, norm2_g, w_router, b_router, w_gate_up, b_gate_up, w_down, b_down, norm_f_g):
    bsz, seq, d = x.shape
    depth = norm1_g.shape[0]
    ident = jnp.ones((d,), F32)
    for layer in range(depth):
        x1, yg, rw_tok = _layer(
            x, norm1_g[layer], w_in[layer], conv_w[layer], dn_a_log[layer], dn_dt_bias[layer],
            dn_norm_g[layer], w_o_dn[layer], sg_ln_g[layer], sg_ln_b[layer], sg_w[layer], sg_b[layer],
            w_o_sg[layer], w_out[layer], norm2_g[layer], w_router[layer], b_router[layer],
            w_gate_up[layer], b_gate_up[layer], w_down[layer], b_down[layer])
        last = layer == depth - 1
        if not last:
            raise NotImplementedError("stacked layers need an un-normalised combine")
        x = _combine_final(x1, yg, rw_tok, norm_f_g if last else ident).reshape(bsz, seq, d)
    return x
```

```python
import functools

import jax
import jax.numpy as jnp
from jax import lax
from jax.experimental import pallas as pl
from jax.experimental.pallas import tpu as pltpu
from jax.experimental.pallas import tpu_sc as plsc

F32 = jnp.float32
BF16 = jnp.bfloat16

DN_HEADS = 4
DN_HEAD_DIM = 128
DN_WIDTH = DN_HEADS * DN_HEAD_DIM
CONV_WIDTH = 4
SG_GROUPS = 4
SG_GROUP_DIM = 128
SG_WIDTH = SG_GROUPS * SG_GROUP_DIM
SG_CHUNK = 128
N_EXPERTS = 32
TOP_K = 4
SWIGLU_LIMIT = 7.0
SWIGLU_ALPHA = 1.702
RMS_EPS = 1e-6
LN_EPS = 1e-5
L2_EPS = 1e-6

OFF_Z = 3 * DN_WIDTH
OFF_BETA = OFF_Z + DN_WIDTH
OFF_U = OFF_BETA + 2 * DN_HEADS

LANES = 128
SUBLANES = 8
SC_CORES = 2
SC_SUBCORES = 16
SC_WORKERS = SC_CORES * SC_SUBCORES
VMEM_LIMIT = 56 * 1024 * 1024

ROW_TILE = 512
GDN_TILE = 512
GDN_CHUNK = 128
MOE_TILE = 512
SC_CHUNK = 64

_NT = (((1,), (1,)), ((), ()))
_TN = (((0,), (0,)), ((), ()))


def _dot(a, b):
    return jnp.dot(a, b, preferred_element_type=F32)


def _dot_nt(a, b):
    return lax.dot_general(a, b, _NT, preferred_element_type=F32)


def _dot_tn(a, b):
    return lax.dot_general(a, b, _TN, preferred_element_type=F32)


def _sigmoid(x):
    return 1.0 / (1.0 + jnp.exp(-x))


def _softplus(x):
    return jnp.maximum(x, 0.0) + jnp.log1p(jnp.exp(-jnp.abs(x)))


def _split3(x):
    hi = x.astype(BF16)
    r = x - hi.astype(F32)
    mid = r.astype(BF16)
    lo = (r - mid.astype(F32)).astype(BF16)
    return hi, mid, lo


def _const_spec(shape):
    nd = len(shape)
    return pl.BlockSpec(shape, lambda *_: (0,) * nd)


def _params(sem):
    return pltpu.CompilerParams(dimension_semantics=sem, vmem_limit_bytes=VMEM_LIMIT)


def _proj_kernel(x_ref, g1_ref, wqkv_ref, wz_ref, wba_ref, wbat_ref, wuv_ref, wga_ref, wgb_ref,
                 alog_r_ref, dtb_r_ref, alog_c_ref, dtb_c_ref, lng_ref, lnb_ref,
                 qkv_ref, zs_ref, bg_ref, bgt_ref, u_ref, v_ref, ga_ref, gb_ref):
    x = x_ref[...]
    ms = jnp.mean(x * x, axis=-1, keepdims=True)
    h = (x * lax.rsqrt(ms + RMS_EPS) * g1_ref[...]).astype(BF16)

    qkv_ref[...] = _dot(h, wqkv_ref[...])

    z = _dot(h, wz_ref[...])
    zs_ref[...] = (z * _sigmoid(z)).astype(BF16)

    ba = _dot(h, wba_ref[...])
    lane = lax.broadcasted_iota(jnp.int32, ba.shape, 1)
    g_col = -jnp.exp(alog_r_ref[...]) * _softplus(ba + dtb_r_ref[...])
    bg_ref[...] = jnp.where(lane < DN_HEADS, _sigmoid(ba), g_col)
    bat = _dot_nt(wbat_ref[...], h)
    row = lax.broadcasted_iota(jnp.int32, bat.shape, 0)
    g_row = -jnp.exp(alog_c_ref[...]) * _softplus(bat + dtb_c_ref[...])
    bgt_ref[...] = jnp.where(row < DN_HEADS, _sigmoid(bat), g_row)

    uv = _dot(h, wuv_ref[...])
    ge = 0.5 * uv * (1.0 + lax.erf(uv * (0.5 ** 0.5)))
    u_ref[...] = ge[:, :SG_WIDTH].astype(BF16)
    v = ge[:, SG_WIDTH:]
    mu = jnp.mean(v, axis=-1, keepdims=True)
    vc = v - mu
    var = jnp.mean(vc * vc, axis=-1, keepdims=True)
    v_ref[...] = (vc * lax.rsqrt(var + LN_EPS) * lng_ref[...] + lnb_ref[...]).astype(BF16)

    ga_ref[...] = _sigmoid(_dot(h, wga_ref[...])).astype(BF16)
    gb_ref[...] = _sigmoid(_dot(h, wgb_ref[...])).astype(BF16)


def _project(x2, norm1_g, w_in, dn_a_log, dn_dt_bias, sg_ln_g, sg_ln_b):
    t, d = x2.shape
    tm = ROW_TILE
    wb = w_in.astype(BF16)
    wqkv = wb[:, :OFF_Z]
    wz = wb[:, OFF_Z:OFF_BETA]
    wba8 = wb[:, OFF_BETA:OFF_U]
    wba = jnp.pad(wba8, ((0, 0), (0, LANES - 2 * DN_HEADS)))
    wbat = wba8.T
    wuv = wb[:, OFF_U:OFF_U + 2 * SG_WIDTH]
    wga = wb[:, OFF_U + 2 * SG_WIDTH:OFF_U + 2 * SG_WIDTH + d]
    wgb = wb[:, OFF_U + 2 * SG_WIDTH + d:]
    zeros4 = jnp.zeros((DN_HEADS,), F32)
    alog8 = jnp.concatenate([zeros4, dn_a_log.astype(F32)])
    dtb8 = jnp.concatenate([zeros4, dn_dt_bias.astype(F32)])
    alog_r = jnp.pad(alog8, (0, LANES - 2 * DN_HEADS)).reshape(1, LANES)
    dtb_r = jnp.pad(dtb8, (0, LANES - 2 * DN_HEADS)).reshape(1, LANES)
    alog_c = alog8.reshape(2 * DN_HEADS, 1)
    dtb_c = dtb8.reshape(2 * DN_HEADS, 1)

    row = lambda w: pl.BlockSpec((tm, w), lambda i: (i, 0))
    out_shape = (
        jax.ShapeDtypeStruct((t, OFF_Z), F32),
        jax.ShapeDtypeStruct((t, DN_WIDTH), BF16),
        jax.ShapeDtypeStruct((t, LANES), F32),
        jax.ShapeDtypeStruct((2 * DN_HEADS, t), F32),
        jax.ShapeDtypeStruct((t, SG_WIDTH), BF16),
        jax.ShapeDtypeStruct((t, SG_WIDTH), BF16),
        jax.ShapeDtypeStruct((t, d), BF16),
        jax.ShapeDtypeStruct((t, d), BF16),
    )
    return pl.pallas_call(
        _proj_kernel,
        out_shape=out_shape,
        grid=(t // tm,),
        in_specs=[
            row(d), _const_spec((1, d)),
            _const_spec(wqkv.shape), _const_spec(wz.shape), _const_spec(wba.shape),
            _const_spec(wbat.shape), _const_spec(wuv.shape), _const_spec(wga.shape),
            _const_spec(wgb.shape),
            _const_spec((1, LANES)), _const_spec((1, LANES)),
            _const_spec((2 * DN_HEADS, 1)), _const_spec((2 * DN_HEADS, 1)),
            _const_spec((1, SG_WIDTH)), _const_spec((1, SG_WIDTH)),
        ],
        out_specs=(
            row(OFF_Z), row(DN_WIDTH), row(LANES),
            pl.BlockSpec((2 * DN_HEADS, tm), lambda i: (0, i)),
            row(SG_WIDTH), row(SG_WIDTH), row(d), row(d),
        ),
        compiler_params=_params(("parallel",)),
        name="in_proj",
    )(x2, norm1_g.reshape(1, d), wqkv, wz, wba, wbat, wuv, wga, wgb,
      alog_r, dtb_r, alog_c, dtb_c, sg_ln_g.reshape(1, SG_WIDTH), sg_ln_b.reshape(1, SG_WIDTH))


_M_DIAG8, _M_OFF16, _M_OFF32, _M_OFF64, _M_OFF128, _M_INCL, _M_STRICT, _M_EYE, _M_UPPER = range(9)


def _gdn_constants(c):
    r = jnp.arange(c)[:, None]
    s = jnp.arange(c)[None, :]
    same = lambda b: (r // b) == (s // b)
    mats = [same(8)]
    b = 8
    while b < c:
        mats.append(same(2 * b) & ~same(b) & (r > s))
        b *= 2
    mats += [r >= s, r > s, r == s, r <= s]
    return jnp.stack(mats).astype(F32)


def _unit_lower_inverse(a_list, cm_ref):
    eye = cm_ref[_M_EYE]
    a0 = [(a * cm_ref[_M_DIAG8]).astype(BF16) for a in a_list]
    a2 = [_dot(x, x).astype(BF16) for x in a0]
    t = [eye - x.astype(F32) for x in a0]
    t = [ti + _dot(ti.astype(BF16), x) for ti, x in zip(t, a2)]
    a4 = [_dot(x, x).astype(BF16) for x in a2]
    t = [ti + _dot(ti.astype(BF16), x) for ti, x in zip(t, a4)]
    n_levels = cm_ref.shape[0] - 5
    for lvl in range(n_levels):
        off = [(a * cm_ref[_M_OFF16 + lvl]).astype(BF16) for a in a_list]
        tb = [ti.astype(BF16) for ti in t]
        x = [_dot(tbi, o).astype(BF16) for tbi, o in zip(tb, off)]
        t = [ti - _dot(xi, tbi) for ti, xi, tbi in zip(t, x, tb)]
    return t


def _gdn_kernel(qkv_ref, bg_ref, bgt_ref, zs_ref, cw_ref, gn_ref, cm_ref, ya_ref,
                xbuf, s_ref, q_s, k_s, v_s):
    c = pl.program_id(1)
    lt = qkv_ref.shape[0]
    ch = cm_ref.shape[1]
    hd = DN_HEAD_DIM
    halo = SUBLANES

    @pl.when(c == 0)
    def _():
        s_ref[...] = jnp.zeros_like(s_ref)
        xbuf[0:halo, :] = jnp.zeros((halo, xbuf.shape[1]), F32)

    @pl.when(c > 0)
    def _():
        xbuf[0:halo, :] = xbuf[lt:lt + halo, :]

    xbuf[halo:lt + halo, :] = qkv_ref[...]
    cw = cw_ref[...]
    conv = cw[CONV_WIDTH - 1:CONV_WIDTH, :] * xbuf[halo:lt + halo, :]
    for kk in range(CONV_WIDTH - 1):
        conv = conv + cw[kk:kk + 1, :] * xbuf[pl.ds(halo - (CONV_WIDTH - 1) + kk, lt), :]
    act = conv * _sigmoid(conv)
    for h in range(DN_HEADS):
        qh = act[:, h * hd:(h + 1) * hd]
        kh = act[:, DN_WIDTH + h * hd:DN_WIDTH + (h + 1) * hd]
        q_s[h] = qh * (lax.rsqrt(jnp.sum(qh * qh, axis=-1, keepdims=True) + L2_EPS) * (hd ** -0.5))
        k_s[h] = kh * lax.rsqrt(jnp.sum(kh * kh, axis=-1, keepdims=True) + L2_EPS)
        v_s[h] = act[:, 2 * DN_WIDTH + h * hd:2 * DN_WIDTH + (h + 1) * hd]

    nc = lt // ch
    lower16 = cm_ref[_M_INCL].astype(BF16)
    upper16 = cm_ref[_M_UPPER].astype(BF16)
    incl = cm_ref[_M_INCL] > 0.0

    bgc, gam, e_gam, e_rest, e_last, gam_t = [], [], [], [], [], []
    for ci in range(nc):
        rows = slice(ci * ch, (ci + 1) * ch)
        b = bg_ref[rows, :]
        hi, mid, lo = _split3(b)
        g = _dot(lower16, hi) + _dot(lower16, mid) + _dot(lower16, lo)
        g_last = g[ch - 1:ch, :]
        bgc.append(b)
        gam.append(g)
        e_gam.append(jnp.exp(g))
        e_rest.append(jnp.exp(g_last - g))
        e_last.append(jnp.exp(g_last))
        hi, mid, lo = _split3(bgt_ref[:, rows])
        gam_t.append(_dot(hi, upper16) + _dot(mid, upper16) + _dot(lo, upper16))

    items = [(ci, h) for ci in range(nc) for h in range(DN_HEADS)]
    a_list, qk16, vb16, kbe16, qe16, kd16 = [], [], [], [], [], []
    for ci, h in items:
        rows = slice(ci * ch, (ci + 1) * ch)
        gl = DN_HEADS + h
        beta = bgc[ci][:, h:h + 1]
        decay = jnp.exp(jnp.where(incl, gam[ci][:, gl:gl + 1] - gam_t[ci][gl:gl + 1, :], -jnp.inf))
        q = q_s[h, rows, :]
        k = k_s[h, rows, :]
        kb = k * beta
        k16 = k.astype(BF16)
        eg = e_gam[ci][:, gl:gl + 1]
        a_list.append(_dot_nt(kb.astype(BF16), k16) * decay * cm_ref[_M_STRICT])
        qk16.append((_dot_nt(q.astype(BF16), k16) * decay).astype(BF16))
        vb16.append((v_s[h, rows, :] * beta).astype(BF16))
        kbe16.append((kb * eg).astype(BF16))
        qe16.append((q * eg).astype(BF16))
        kd16.append((k * e_rest[ci][:, gl:gl + 1]).astype(BF16))
    t16 = [t.astype(BF16) for t in _unit_lower_inverse(a_list, cm_ref)]
    u_list = [_dot(t, x) for t, x in zip(t16, vb16)]
    w16 = [_dot(t, x).astype(BF16) for t, x in zip(t16, kbe16)]

    gn = gn_ref[...]
    state = [s_ref[h] for h in range(DN_HEADS)]
    for ci in range(nc):
        rows = slice(ci * ch, (ci + 1) * ch)
        idx = [ci * DN_HEADS + h for h in range(DN_HEADS)]
        st16 = [st.astype(BF16) for st in state]
        vn16 = [(u_list[i] - _dot(w16[i], st)).astype(BF16) for i, st in zip(idx, st16)]
        o = [_dot(qe16[i], st) + _dot(qk16[i], vn) for i, st, vn in zip(idx, st16, vn16)]
        state = [st * e_last[ci][:, DN_HEADS + h:DN_HEADS + h + 1] + _dot_tn(kd16[i], vn)
                 for h, (i, st, vn) in enumerate(zip(idx, state, vn16))]
        for h in range(DN_HEADS):
            oh = o[h]
            on = oh * lax.rsqrt(jnp.mean(oh * oh, axis=-1, keepdims=True) + RMS_EPS) * gn
            zs = zs_ref[rows, h * hd:(h + 1) * hd].astype(F32)
            ya_ref[rows, h * hd:(h + 1) * hd] = (on * zs).astype(BF16)
    for h in range(DN_HEADS):
        s_ref[h] = state[h]


def _gated_delta(qkv, bg, bgt, zs, conv_w, dn_norm_g, batch, seq):
    t = batch * seq
    lt = GDN_TILE
    nt = seq // lt
    cm = _gdn_constants(GDN_CHUNK)
    w3 = qkv.shape[1]
    row = lambda w: pl.BlockSpec((lt, w), lambda b, c: (b * nt + c, 0))
    return pl.pallas_call(
        _gdn_kernel,
        out_shape=jax.ShapeDtypeStruct((t, DN_WIDTH), BF16),
        grid=(batch, nt),
        in_specs=[
            row(w3), row(LANES),
            pl.BlockSpec((2 * DN_HEADS, lt), lambda b, c: (0, b * nt + c)),
            row(DN_WIDTH),
            _const_spec(conv_w.shape), _const_spec((1, DN_HEAD_DIM)), _const_spec(cm.shape),
        ],
        out_specs=row(DN_WIDTH),
        scratch_shapes=[
            pltpu.VMEM((lt + SUBLANES, w3), F32),
            pltpu.VMEM((DN_HEADS, DN_HEAD_DIM, DN_HEAD_DIM), F32),
            pltpu.VMEM((DN_HEADS, lt, DN_HEAD_DIM), F32),
            pltpu.VMEM((DN_HEADS, lt, DN_HEAD_DIM), F32),
            pltpu.VMEM((DN_HEADS, lt, DN_HEAD_DIM), F32),
        ],
        compiler_params=_params(("parallel", "arbitrary")),
        name="gated_delta",
    )(qkv, bg, bgt, zs, conv_w.astype(F32), dn_norm_g.reshape(1, DN_HEAD_DIM).astype(F32), cm)


def _merge_kernel(x_ref, ya_ref, u_ref, v_ref, ga_ref, gb_ref, sgw_ref, sgb_ref, wdn_ref, wsg_ref,
                  wout_ref, g2_ref, wrt_ref, br_ref,
                  x1_ref, h2_ref, route_ref, rw_ref, cnt_ref,
                  yb_s, carry_s):
    i = pl.program_id(0)
    tm = x_ref.shape[0]
    ne = wrt_ref.shape[0]

    @pl.when(i == 0)
    def _():
        carry_s[...] = jnp.zeros_like(carry_s)

    tri = (lax.broadcasted_iota(jnp.int32, (SG_CHUNK, SG_CHUNK), 0)
           >= lax.broadcasted_iota(jnp.int32, (SG_CHUNK, SG_CHUNK), 1))
    for g in range(SG_GROUPS):
        wg = jnp.where(tri, sgw_ref[g], 0.0).astype(BF16)
        cols = slice(g * SG_GROUP_DIM, (g + 1) * SG_GROUP_DIM)
        for n in range(tm // SG_CHUNK):
            rows = slice(n * SG_CHUNK, (n + 1) * SG_CHUNK)
            mix = _dot(wg, v_ref[rows, cols]) + sgb_ref[:, cols]
            yb_s[rows, cols] = (u_ref[rows, cols].astype(F32) * mix).astype(BF16)

    da = _dot(ya_ref[...], wdn_ref[...])
    db = _dot(yb_s[...], wsg_ref[...])
    merged = ga_ref[...].astype(F32) * da + gb_ref[...].astype(F32) * db
    x1 = x_ref[...] + _dot(merged.astype(BF16), wout_ref[...])
    x1_ref[...] = x1
    ms = jnp.mean(x1 * x1, axis=-1, keepdims=True)
    h2 = x1 * lax.rsqrt(ms + RMS_EPS) * g2_ref[...]
    h2_ref[...] = h2

    logits = _dot_nt(wrt_ref[...], h2.astype(BF16)) + br_ref[...]
    eidx = lax.broadcasted_iota(jnp.int32, (ne, tm), 0).astype(F32)
    work = logits
    onehot = jnp.zeros((ne, tm), F32)
    sels, tops, idxs = [], [], []
    for _ in range(TOP_K):
        m = jnp.max(work, axis=0, keepdims=True)
        idx = jnp.min(jnp.where(work == m, eidx, float(ne)), axis=0, keepdims=True)
        sel = eidx == idx
        work = jnp.where(sel, -jnp.inf, work)
        onehot = onehot + sel.astype(F32)
        sels.append(sel)
        tops.append(m)
        idxs.append(idx)
    ps = [jnp.exp(m - tops[0]) for m in tops]
    den = ps[0] + ps[1] + ps[2] + ps[3]
    before = (lax.broadcasted_iota(jnp.int32, (tm, tm), 0)
              < lax.broadcasted_iota(jnp.int32, (tm, tm), 1)).astype(BF16)
    cum = _dot(onehot.astype(BF16), before) + carry_s[:, 0:1]
    ranks = [jnp.sum(jnp.where(sel, cum, 0.0), axis=0, keepdims=True) for sel in sels]
    route_ref[...] = jnp.concatenate(idxs + ranks, axis=0).astype(jnp.int32)
    rw_ref[...] = jnp.concatenate([p / den for p in ps] + [jnp.zeros((TOP_K, tm), F32)], axis=0)
    carry_s[...] = carry_s[...] + jnp.sum(onehot, axis=1, keepdims=True)
    cnt_ref[...] = carry_s[...]


def _merge_route(x2, ya, u, v, ga, gb, sg_w, sg_b, w_o_dn, w_o_sg, w_out, norm2_g, w_router, b_router):
    t, d = x2.shape
    tm = ROW_TILE
    ne = w_router.shape[1]
    sgb_full = jnp.repeat(sg_b.T.astype(F32), SG_GROUP_DIM, axis=1)
    row = lambda w: pl.BlockSpec((tm, w), lambda i: (i, 0))
    col = lambda h: pl.BlockSpec((h, tm), lambda i: (0, i))
    out_shape = (
        jax.ShapeDtypeStruct((t, d), F32),
        jax.ShapeDtypeStruct((t, d), F32),
        jax.ShapeDtypeStruct((2 * TOP_K, t), jnp.int32),
        jax.ShapeDtypeStruct((2 * TOP_K, t), F32),
        jax.ShapeDtypeStruct((ne, LANES), F32),
    )
    return pl.pallas_call(
        _merge_kernel,
        out_shape=out_shape,
        grid=(t // tm,),
        in_specs=[
            row(d), row(DN_WIDTH), row(SG_WIDTH), row(SG_WIDTH), row(d), row(d),
            _const_spec(sg_w.shape), _const_spec(sgb_full.shape),
            _const_spec(w_o_dn.shape), _const_spec(w_o_sg.shape), _const_spec(w_out.shape),
            _const_spec((1, d)), _const_spec((ne, d)), _const_spec((ne, 1)),
        ],
        out_specs=(row(d), row(d), col(2 * TOP_K), col(2 * TOP_K), _const_spec((ne, LANES))),
        scratch_shapes=[pltpu.VMEM((tm, SG_WIDTH), BF16), pltpu.VMEM((ne, LANES), F32)],
        compiler_params=_params(("arbitrary",)),
        name="merge_route",
    )(x2, ya, u, v, ga, gb, sg_w.astype(F32), sgb_full, w_o_dn.astype(BF16), w_o_sg.astype(BF16),
      w_out.astype(BF16), norm2_g.reshape(1, d).astype(F32), w_router.T.astype(BF16),
      b_router.reshape(ne, 1).astype(F32))


def _sc_mesh():
    return plsc.VectorSubcoreMesh(core_axis_name="c", subcore_axis_name="s")


def _dispatch(h2, dest3, n_rows):
    n_tok, d = h2.shape
    chunk = dest3.shape[2]
    per_w = n_tok // SC_WORKERS
    n_chunks = per_w // chunk

    @functools.partial(
        pl.kernel, mesh=_sc_mesh(),
        out_type=jax.ShapeDtypeStruct((n_rows, d), h2.dtype),
        scratch_types=[pltpu.VMEM((TOP_K, chunk), jnp.int32), pltpu.VMEM((chunk, d), h2.dtype)],
        name="moe_dispatch",
    )
    def k(h2_hbm, dest_hbm, xs_hbm, idx_v, rows_v):
        wid = lax.axis_index("s") * SC_CORES + lax.axis_index("c")
        base = wid * per_w

        @pl.loop(0, n_chunks)
        def _(j):
            t0 = pl.multiple_of(base + j * chunk, chunk)
            pltpu.sync_copy(dest_hbm.at[base // chunk + j], idx_v)
            pltpu.sync_copy(h2_hbm.at[pl.ds(t0, chunk)], rows_v)
            for kk in range(TOP_K):
                pltpu.sync_copy(rows_v, xs_hbm.at[idx_v.at[kk]])

    return k(h2, dest3)


def _collect(y, dest3):
    n_tok = dest3.shape[0] * dest3.shape[2]
    chunk = dest3.shape[2]
    d = y.shape[1]
    per_w = n_tok // SC_WORKERS
    n_chunks = per_w // chunk

    @functools.partial(
        pl.kernel, mesh=_sc_mesh(),
        out_type=jax.ShapeDtypeStruct((TOP_K, n_tok, d), y.dtype),
        scratch_types=[pltpu.VMEM((TOP_K, chunk), jnp.int32), pltpu.VMEM((chunk, d), y.dtype)],
        name="moe_collect",
    )
    def k(y_hbm, dest_hbm, yg_hbm, idx_v, rows_v):
        wid = lax.axis_index("s") * SC_CORES + lax.axis_index("c")
        base = wid * per_w

        @pl.loop(0, n_chunks)
        def _(j):
            t0 = pl.multiple_of(base + j * chunk, chunk)
            pltpu.sync_copy(dest_hbm.at[base // chunk + j], idx_v)
            for kk in range(TOP_K):
                pltpu.sync_copy(y_hbm.at[idx_v.at[kk]], rows_v)
                pltpu.sync_copy(rows_v, yg_hbm.at[kk, pl.ds(t0, chunk)])

    return k(y, dest3)


def _expert_kernel(be_ref, nv_ref, xs_ref, wgu_ref, bgu_ref, wd_ref, bd_ref, y_ref, wgu_s, wd_s):
    i = pl.program_id(0)
    dff = wd_ref.shape[1]

    @pl.when(jnp.logical_or(i == 0, be_ref[i] != be_ref[jnp.maximum(i - 1, 0)]))
    def _():
        wgu_s[...] = wgu_ref[0].astype(BF16)
        wd_s[...] = wd_ref[0].astype(BF16)

    @pl.when(i < nv_ref[0])
    def _():
        gu = _dot(xs_ref[...].astype(BF16), wgu_s[...]) + bgu_ref[0]
        x_glu = jnp.minimum(gu[:, :dff], SWIGLU_LIMIT)
        x_lin = jnp.clip(gu[:, dff:], -SWIGLU_LIMIT, SWIGLU_LIMIT)
        act = x_glu * _sigmoid(SWIGLU_ALPHA * x_glu) * (x_lin + 1.0)
        y_ref[...] = _dot(act.astype(BF16), wd_s[...]) + bd_ref[0]

    @pl.when(i >= nv_ref[0])
    def _():
        y_ref[...] = jnp.zeros_like(y_ref)


def _expert_ffn(xs, block_expert, n_valid, w_gate_up, b_gate_up, w_down, b_down):
    n_rows, d = xs.shape
    tm = MOE_TILE
    ne, _, dff2 = w_gate_up.shape
    dff = dff2 // 2
    n_blocks = n_rows // tm
    grid_spec = pltpu.PrefetchScalarGridSpec(
        num_scalar_prefetch=2,
        grid=(n_blocks,),
        in_specs=[
            pl.BlockSpec((tm, d), lambda i, be, nv: (jnp.minimum(i, nv[0] - 1), 0)),
            pl.BlockSpec((1, d, dff2), lambda i, be, nv: (be[i], 0, 0)),
            pl.BlockSpec((1, 1, dff2), lambda i, be, nv: (be[i], 0, 0)),
            pl.BlockSpec((1, dff, d), lambda i, be, nv: (be[i], 0, 0)),
            pl.BlockSpec((1, 1, d), lambda i, be, nv: (be[i], 0, 0)),
        ],
        out_specs=pl.BlockSpec((tm, d), lambda i, be, nv: (i, 0)),
        scratch_shapes=[pltpu.VMEM((d, dff2), BF16), pltpu.VMEM((dff, d), BF16)],
    )
    return pl.pallas_call(
        _expert_kernel,
        out_shape=jax.ShapeDtypeStruct((n_rows, d), F32),
        grid_spec=grid_spec,
        compiler_params=_params(("arbitrary",)),
        name="expert_ffn",
    )(block_expert, n_valid, xs, w_gate_up, b_gate_up.reshape(ne, 1, dff2).astype(F32),
      w_down, b_down.reshape(ne, 1, d).astype(F32))


def _final_kernel(x1_ref, yg_ref, rw_ref, gf_ref, o_ref):
    acc = x1_ref[...]
    rw = rw_ref[...]
    for kk in range(TOP_K):
        acc = acc + rw[:, kk:kk + 1] * yg_ref[kk]
    ms = jnp.mean(acc * acc, axis=-1, keepdims=True)
    o_ref[...] = acc * lax.rsqrt(ms + RMS_EPS) * gf_ref[...]


def _combine_final(x1, yg, rw_tok, norm_f_g):
    t, d = x1.shape
    tm = ROW_TILE
    return pl.pallas_call(
        _final_kernel,
        out_shape=jax.ShapeDtypeStruct((t, d), F32),
        grid=(t // tm,),
        in_specs=[
            pl.BlockSpec((tm, d), lambda i: (i, 0)),
            pl.BlockSpec((TOP_K, tm, d), lambda i: (0, i, 0)),
            pl.BlockSpec((tm, 2 * TOP_K), lambda i: (i, 0)),
            _const_spec((1, d)),
        ],
        out_specs=pl.BlockSpec((tm, d), lambda i: (i, 0)),
        compiler_params=_params(("parallel",)),
        name="combine_final",
    )(x1, yg, rw_tok, norm_f_g.reshape(1, d).astype(F32))


def _routing_tables(route, counts, n_tok):
    tm = MOE_TILE
    n_blocks = (n_tok * TOP_K) // tm + N_EXPERTS
    cnt = counts[:, 0].astype(jnp.int32)
    padded = (cnt + tm - 1) // tm * tm
    pad_end = jnp.cumsum(padded)
    pad_start = pad_end - padded
    eid = route[:TOP_K]
    rank = route[TOP_K:]
    start = jnp.sum(jnp.where(eid[..., None] == jnp.arange(N_EXPERTS, dtype=jnp.int32),
                              pad_start, 0), axis=-1)
    dest = (start + rank).astype(jnp.int32)
    dest3 = dest.reshape(TOP_K, n_tok // SC_CHUNK, SC_CHUNK).transpose(1, 0, 2)
    n_valid = (pad_end[-1] // tm).astype(jnp.int32)
    blk = jnp.arange(n_blocks, dtype=jnp.int32)
    first_row = jnp.minimum(blk, n_valid - 1) * tm
    be = jnp.sum((pad_end[None, :] <= first_row[:, None]).astype(jnp.int32), axis=1)
    be = jnp.minimum(be, N_EXPERTS - 1).astype(jnp.int32)
    return dest3, be, n_valid.reshape(1), n_blocks * tm


def _layer(x, norm1_g, w_in, conv_w, dn_a_log, dn_dt_bias, dn_norm_g, w_o_dn, sg_ln_g, sg_ln_b,
           sg_w, sg_b, w_o_sg, w_out, norm2_g, w_router, b_router, w_gate_up, b_gate_up, w_down, b_down):
    bsz, seq, d = x.shape
    t = bsz * seq
    x2 = x.reshape(t, d)
    qkv, zs, bg, bgt, u, v, ga, gb = _project(x2, norm1_g, w_in, dn_a_log, dn_dt_bias, sg_ln_g, sg_ln_b)
    ya = _gated_delta(qkv, bg, bgt, zs, conv_w, dn_norm_g, bsz, seq)
    x1, h2, route, rw, counts = _merge_route(x2, ya, u, v, ga, gb, sg_w, sg_b, w_o_dn, w_o_sg, w_out,
                                             norm2_g, w_router, b_router)
    dest3, block_expert, n_valid, n_rows = _routing_tables(route, counts, t)
    xs = _dispatch(h2, dest3, n_rows)
    y = _expert_ffn(xs, block_expert, n_valid, w_gate_up, b_gate_up, w_down, b_down)
    yg = _collect(y, dest3)
    return x1, yg, rw.T


def kernel(x, norm1_g, w_in, conv_w, dn_a_log, dn_dt_bias, dn_norm_g, w_o_dn, sg_ln_g, sg_ln_b, sg_w, sg_b, w_o_sg, w_out, norm2_g, w_router, b_router, w_gate_up, b_gate_up, w_down, b_down, norm_f_g):
    bsz, seq, d = x.shape
    depth = norm1_g.shape[0]
    ident = jnp.ones((d,), F32)
    for layer in range(depth):
        x1, yg, rw_tok = _layer(
            x, norm1_g[layer], w_in[layer], conv_w[layer], dn_a_log[layer], dn_dt_bias[layer],
            dn_norm_g[layer], w_o_dn[layer], sg_ln_g[layer], sg_ln_b[layer], sg_w[layer], sg_b[layer],
            w_o_sg[layer], w_out[layer], norm2_g[layer], w_router[layer], b_router[layer],
            w_gate_up[layer], b_gate_up[layer], w_down[layer], b_down[layer])
        last = layer == depth - 1
        if not last:
            raise NotImplementedError("stacked layers need an un-normalised combine")
        x = _combine_final(x1, yg, rw_tok, norm_f_g if last else ident).reshape(bsz, seq, d)
    return x
```

```python
import functools

import jax
import jax.numpy as jnp
from jax import lax
from jax.experimental import pallas as pl
from jax.experimental.pallas import tpu as pltpu
from jax.experimental.pallas import tpu_sc as plsc

F32 = jnp.float32
BF16 = jnp.bfloat16

DN_HEADS = 4
DN_HEAD_DIM = 128
DN_WIDTH = DN_HEADS * DN_HEAD_DIM
CONV_WIDTH = 4
SG_GROUPS = 4
SG_GROUP_DIM = 128
SG_WIDTH = SG_GROUPS * SG_GROUP_DIM
SG_CHUNK = 128
N_EXPERTS = 32
TOP_K = 4
SWIGLU_LIMIT = 7.0
SWIGLU_ALPHA = 1.702
RMS_EPS = 1e-6
LN_EPS = 1e-5
L2_EPS = 1e-6

OFF_Z = 3 * DN_WIDTH
OFF_BETA = OFF_Z + DN_WIDTH
OFF_U = OFF_BETA + 2 * DN_HEADS

LANES = 128
SUBLANES = 8
SC_CORES = 2
SC_SUBCORES = 16
SC_WORKERS = SC_CORES * SC_SUBCORES
VMEM_LIMIT = 56 * 1024 * 1024

ROW_TILE = 512
GDN_TILE = 512
GDN_CHUNK = 128
MOE_TILE = 512
SC_CHUNK = 64

_NT = (((1,), (1,)), ((), ()))
_TN = (((0,), (0,)), ((), ()))


def _dot(a, b):
    return jnp.dot(a, b, preferred_element_type=F32)


def _dot_nt(a, b):
    return lax.dot_general(a, b, _NT, preferred_element_type=F32)


def _dot_tn(a, b):
    return lax.dot_general(a, b, _TN, preferred_element_type=F32)


def _sigmoid(x):
    return 1.0 / (1.0 + jnp.exp(-x))


def _softplus(x):
    return jnp.maximum(x, 0.0) + jnp.log1p(jnp.exp(-jnp.abs(x)))


def _split3(x):
    hi = x.astype(BF16)
    r = x - hi.astype(F32)
    mid = r.astype(BF16)
    lo = (r - mid.astype(F32)).astype(BF16)
    return hi, mid, lo


def _pack_bf16_pairs(x):
    n = x.shape[1] // 2
    lo = lax.bitcast_convert_type(x[:, :n].astype(BF16).astype(F32), jnp.int32)
    hi = lax.bitcast_convert_type(x[:, n:].astype(BF16).astype(F32), jnp.int32)
    return lax.shift_right_logical(lo, 16) | (hi & jnp.int32(-65536))


def _unpack_bf16_pairs(p):
    lo = lax.bitcast_convert_type(lax.shift_left(p, 16), F32)
    hi = lax.bitcast_convert_type(p & jnp.int32(-65536), F32)
    return jnp.concatenate([lo, hi], axis=1)


def _const_spec(shape):
    nd = len(shape)
    return pl.BlockSpec(shape, lambda *_: (0,) * nd)


def _params(sem):
    return pltpu.CompilerParams(dimension_semantics=sem, vmem_limit_bytes=VMEM_LIMIT)


def _proj_kernel(x_ref, g1_ref, wqkv_ref, wz_ref, wba_ref, wbat_ref, wuv_ref, wga_ref, wgb_ref,
                 alog_r_ref, dtb_r_ref, alog_c_ref, dtb_c_ref, lng_ref, lnb_ref,
                 qkv_ref, zs_ref, bg_ref, bgt_ref, u_ref, v_ref, ga_ref, gb_ref):
    x = x_ref[...]
    ms = jnp.mean(x * x, axis=-1, keepdims=True)
    h = (x * lax.rsqrt(ms + RMS_EPS) * g1_ref[...]).astype(BF16)

    qkv_ref[...] = _dot(h, wqkv_ref[...])

    z = _dot(h, wz_ref[...])
    zs_ref[...] = (z * _sigmoid(z)).astype(BF16)

    ba = _dot(h, wba_ref[...])
    lane = lax.broadcasted_iota(jnp.int32, ba.shape, 1)
    g_col = -jnp.exp(alog_r_ref[...]) * _softplus(ba + dtb_r_ref[...])
    bg_ref[...] = jnp.where(lane < DN_HEADS, _sigmoid(ba), g_col)
    bat = _dot_nt(wbat_ref[...], h)
    row = lax.broadcasted_iota(jnp.int32, bat.shape, 0)
    g_row = -jnp.exp(alog_c_ref[...]) * _softplus(bat + dtb_c_ref[...])
    bgt_ref[...] = jnp.where(row < DN_HEADS, _sigmoid(bat), g_row)

    uv = _dot(h, wuv_ref[...])
    ge = 0.5 * uv * (1.0 + lax.erf(uv * (0.5 ** 0.5)))
    u_ref[...] = ge[:, :SG_WIDTH].astype(BF16)
    v = ge[:, SG_WIDTH:]
    mu = jnp.mean(v, axis=-1, keepdims=True)
    vc = v - mu
    var = jnp.mean(vc * vc, axis=-1, keepdims=True)
    v_ref[...] = (vc * lax.rsqrt(var + LN_EPS) * lng_ref[...] + lnb_ref[...]).astype(BF16)

    ga_ref[...] = _sigmoid(_dot(h, wga_ref[...])).astype(BF16)
    gb_ref[...] = _sigmoid(_dot(h, wgb_ref[...])).astype(BF16)


def _project(x2, norm1_g, w_in, dn_a_log, dn_dt_bias, sg_ln_g, sg_ln_b):
    t, d = x2.shape
    tm = ROW_TILE
    wb = w_in.astype(BF16)
    wqkv = wb[:, :OFF_Z]
    wz = wb[:, OFF_Z:OFF_BETA]
    wba8 = wb[:, OFF_BETA:OFF_U]
    wba = jnp.pad(wba8, ((0, 0), (0, LANES - 2 * DN_HEADS)))
    wbat = wba8.T
    wuv = wb[:, OFF_U:OFF_U + 2 * SG_WIDTH]
    wga = wb[:, OFF_U + 2 * SG_WIDTH:OFF_U + 2 * SG_WIDTH + d]
    wgb = wb[:, OFF_U + 2 * SG_WIDTH + d:]
    zeros4 = jnp.zeros((DN_HEADS,), F32)
    alog8 = jnp.concatenate([zeros4, dn_a_log.astype(F32)])
    dtb8 = jnp.concatenate([zeros4, dn_dt_bias.astype(F32)])
    alog_r = jnp.pad(alog8, (0, LANES - 2 * DN_HEADS)).reshape(1, LANES)
    dtb_r = jnp.pad(dtb8, (0, LANES - 2 * DN_HEADS)).reshape(1, LANES)
    alog_c = alog8.reshape(2 * DN_HEADS, 1)
    dtb_c = dtb8.reshape(2 * DN_HEADS, 1)

    row = lambda w: pl.BlockSpec((tm, w), lambda i: (i, 0))
    out_shape = (
        jax.ShapeDtypeStruct((t, OFF_Z), F32),
        jax.ShapeDtypeStruct((t, DN_WIDTH), BF16),
        jax.ShapeDtypeStruct((t, LANES), F32),
        jax.ShapeDtypeStruct((2 * DN_HEADS, t), F32),
        jax.ShapeDtypeStruct((t, SG_WIDTH), BF16),
        jax.ShapeDtypeStruct((t, SG_WIDTH), BF16),
        jax.ShapeDtypeStruct((t, d), BF16),
        jax.ShapeDtypeStruct((t, d), BF16),
    )
    return pl.pallas_call(
        _proj_kernel,
        out_shape=out_shape,
        grid=(t // tm,),
        in_specs=[
            row(d), _const_spec((1, d)),
            _const_spec(wqkv.shape), _const_spec(wz.shape), _const_spec(wba.shape),
            _const_spec(wbat.shape), _const_spec(wuv.shape), _const_spec(wga.shape),
            _const_spec(wgb.shape),
            _const_spec((1, LANES)), _const_spec((1, LANES)),
            _const_spec((2 * DN_HEADS, 1)), _const_spec((2 * DN_HEADS, 1)),
            _const_spec((1, SG_WIDTH)), _const_spec((1, SG_WIDTH)),
        ],
        out_specs=(
            row(OFF_Z), row(DN_WIDTH), row(LANES),
            pl.BlockSpec((2 * DN_HEADS, tm), lambda i: (0, i)),
            row(SG_WIDTH), row(SG_WIDTH), row(d), row(d),
        ),
        compiler_params=_params(("parallel",)),
        name="in_proj",
    )(x2, norm1_g.reshape(1, d), wqkv, wz, wba, wbat, wuv, wga, wgb,
      alog_r, dtb_r, alog_c, dtb_c, sg_ln_g.reshape(1, SG_WIDTH), sg_ln_b.reshape(1, SG_WIDTH))


_M_DIAG8, _M_OFF16, _M_OFF32, _M_OFF64, _M_OFF128, _M_INCL, _M_STRICT, _M_EYE, _M_UPPER = range(9)


def _gdn_constants(c):
    r = jnp.arange(c)[:, None]
    s = jnp.arange(c)[None, :]
    same = lambda b: (r // b) == (s // b)
    mats = [same(8)]
    b = 8
    while b < c:
        mats.append(same(2 * b) & ~same(b) & (r > s))
        b *= 2
    mats += [r >= s, r > s, r == s, r <= s]
    return jnp.stack(mats).astype(F32)


def _unit_lower_inverse(a_list, cm_ref):
    eye = cm_ref[_M_EYE]
    a0 = [(a * cm_ref[_M_DIAG8]).astype(BF16) for a in a_list]
    a2 = [_dot(x, x).astype(BF16) for x in a0]
    t = [eye - x.astype(F32) for x in a0]
    t = [ti + _dot(ti.astype(BF16), x) for ti, x in zip(t, a2)]
    a4 = [_dot(x, x).astype(BF16) for x in a2]
    t = [ti + _dot(ti.astype(BF16), x) for ti, x in zip(t, a4)]
    n_levels = cm_ref.shape[0] - 5
    for lvl in range(n_levels):
        off = [(a * cm_ref[_M_OFF16 + lvl]).astype(BF16) for a in a_list]
        tb = [ti.astype(BF16) for ti in t]
        x = [_dot(tbi, o).astype(BF16) for tbi, o in zip(tb, off)]
        t = [ti - _dot(xi, tbi) for ti, xi, tbi in zip(t, x, tb)]
    return t


def _gdn_kernel(qkv_ref, bg_ref, bgt_ref, zs_ref, cw_ref, gn_ref, cm_ref, ya_ref,
                xbuf, s_ref, q_s, k_s, v_s):
    c = pl.program_id(1)
    lt = qkv_ref.shape[0]
    ch = cm_ref.shape[1]
    hd = DN_HEAD_DIM
    halo = SUBLANES

    @pl.when(c == 0)
    def _():
        s_ref[...] = jnp.zeros_like(s_ref)
        xbuf[0:halo, :] = jnp.zeros((halo, xbuf.shape[1]), F32)

    @pl.when(c > 0)
    def _():
        xbuf[0:halo, :] = xbuf[lt:lt + halo, :]

    xbuf[halo:lt + halo, :] = qkv_ref[...]
    cw = cw_ref[...]
    conv = cw[CONV_WIDTH - 1:CONV_WIDTH, :] * xbuf[halo:lt + halo, :]
    for kk in range(CONV_WIDTH - 1):
        conv = conv + cw[kk:kk + 1, :] * xbuf[pl.ds(halo - (CONV_WIDTH - 1) + kk, lt), :]
    act = conv * _sigmoid(conv)
    for h in range(DN_HEADS):
        qh = act[:, h * hd:(h + 1) * hd]
        kh = act[:, DN_WIDTH + h * hd:DN_WIDTH + (h + 1) * hd]
        q_s[h] = qh * (lax.rsqrt(jnp.sum(qh * qh, axis=-1, keepdims=True) + L2_EPS) * (hd ** -0.5))
        k_s[h] = kh * lax.rsqrt(jnp.sum(kh * kh, axis=-1, keepdims=True) + L2_EPS)
        v_s[h] = act[:, 2 * DN_WIDTH + h * hd:2 * DN_WIDTH + (h + 1) * hd]

    nc = lt // ch
    lower16 = cm_ref[_M_INCL].astype(BF16)
    upper16 = cm_ref[_M_UPPER].astype(BF16)
    incl = cm_ref[_M_INCL] > 0.0

    bgc, gam, e_gam, e_rest, e_last, gam_t = [], [], [], [], [], []
    for ci in range(nc):
        rows = slice(ci * ch, (ci + 1) * ch)
        b = bg_ref[rows, :]
        hi, mid, lo = _split3(b)
        g = _dot(lower16, hi) + _dot(lower16, mid) + _dot(lower16, lo)
        g_last = g[ch - 1:ch, :]
        bgc.append(b)
        gam.append(g)
        e_gam.append(jnp.exp(g))
        e_rest.append(jnp.exp(g_last - g))
        e_last.append(jnp.exp(g_last))
        hi, mid, lo = _split3(bgt_ref[:, rows])
        gam_t.append(_dot(hi, upper16) + _dot(mid, upper16) + _dot(lo, upper16))

    items = [(ci, h) for ci in range(nc) for h in range(DN_HEADS)]
    a_list, qk16, vb16, kbe16, qe16, kd16 = [], [], [], [], [], []
    for ci, h in items:
        rows = slice(ci * ch, (ci + 1) * ch)
        gl = DN_HEADS + h
        beta = bgc[ci][:, h:h + 1]
        decay = jnp.exp(jnp.where(incl, gam[ci][:, gl:gl + 1] - gam_t[ci][gl:gl + 1, :], -jnp.inf))
        q = q_s[h, rows, :]
        k = k_s[h, rows, :]
        kb = k * beta
        k16 = k.astype(BF16)
        eg = e_gam[ci][:, gl:gl + 1]
        a_list.append(_dot_nt(kb.astype(BF16), k16) * decay * cm_ref[_M_STRICT])
        qk16.append((_dot_nt(q.astype(BF16), k16) * decay).astype(BF16))
        vb16.append((v_s[h, rows, :] * beta).astype(BF16))
        kbe16.append((kb * eg).astype(BF16))
        qe16.append((q * eg).astype(BF16))
        kd16.append((k * e_rest[ci][:, gl:gl + 1]).astype(BF16))
    t16 = [t.astype(BF16) for t in _unit_lower_inverse(a_list, cm_ref)]
    u_list = [_dot(t, x) for t, x in zip(t16, vb16)]
    w16 = [_dot(t, x).astype(BF16) for t, x in zip(t16, kbe16)]

    gn = gn_ref[...]
    state = [s_ref[h] for h in range(DN_HEADS)]
    for ci in range(nc):
        rows = slice(ci * ch, (ci + 1) * ch)
        idx = [ci * DN_HEADS + h for h in range(DN_HEADS)]
        st16 = [st.astype(BF16) for st in state]
        vn16 = [(u_list[i] - _dot(w16[i], st)).astype(BF16) for i, st in zip(idx, st16)]
        o = [_dot(qe16[i], st) + _dot(qk16[i], vn) for i, st, vn in zip(idx, st16, vn16)]
        state = [st * e_last[ci][:, DN_HEADS + h:DN_HEADS + h + 1] + _dot_tn(kd16[i], vn)
                 for h, (i, st, vn) in enumerate(zip(idx, state, vn16))]
        for h in range(DN_HEADS):
            oh = o[h]
            on = oh * lax.rsqrt(jnp.mean(oh * oh, axis=-1, keepdims=True) + RMS_EPS) * gn
            zs = zs_ref[rows, h * hd:(h + 1) * hd].astype(F32)
            ya_ref[rows, h * hd:(h + 1) * hd] = (on * zs).astype(BF16)
    for h in range(DN_HEADS):
        s_ref[h] = state[h]


def _gated_delta(qkv, bg, bgt, zs, conv_w, dn_norm_g, batch, seq):
    t = batch * seq
    lt = GDN_TILE
    nt = seq // lt
    cm = _gdn_constants(GDN_CHUNK)
    w3 = qkv.shape[1]
    row = lambda w: pl.BlockSpec((lt, w), lambda b, c: (b * nt + c, 0))
    return pl.pallas_call(
        _gdn_kernel,
        out_shape=jax.ShapeDtypeStruct((t, DN_WIDTH), BF16),
        grid=(batch, nt),
        in_specs=[
            row(w3), row(LANES),
            pl.BlockSpec((2 * DN_HEADS, lt), lambda b, c: (0, b * nt + c)),
            row(DN_WIDTH),
            _const_spec(conv_w.shape), _const_spec((1, DN_HEAD_DIM)), _const_spec(cm.shape),
        ],
        out_specs=row(DN_WIDTH),
        scratch_shapes=[
            pltpu.VMEM((lt + SUBLANES, w3), F32),
            pltpu.VMEM((DN_HEADS, DN_HEAD_DIM, DN_HEAD_DIM), F32),
            pltpu.VMEM((DN_HEADS, lt, DN_HEAD_DIM), F32),
            pltpu.VMEM((DN_HEADS, lt, DN_HEAD_DIM), F32),
            pltpu.VMEM((DN_HEADS, lt, DN_HEAD_DIM), F32),
        ],
        compiler_params=_params(("parallel", "arbitrary")),
        name="gated_delta",
    )(qkv, bg, bgt, zs, conv_w.astype(F32), dn_norm_g.reshape(1, DN_HEAD_DIM).astype(F32), cm)


def _merge_kernel(x_ref, ya_ref, u_ref, v_ref, ga_ref, gb_ref, sgw_ref, sgb_ref, wdn_ref, wsg_ref,
                  wout_ref, g2_ref, wrt_ref, br_ref,
                  x1_ref, h2_ref, route_ref, rw_ref, cnt_ref,
                  yb_s, carry_s):
    i = pl.program_id(0)
    tm = x_ref.shape[0]
    ne = wrt_ref.shape[0]

    @pl.when(i == 0)
    def _():
        carry_s[...] = jnp.zeros_like(carry_s)

    tri = (lax.broadcasted_iota(jnp.int32, (SG_CHUNK, SG_CHUNK), 0)
           >= lax.broadcasted_iota(jnp.int32, (SG_CHUNK, SG_CHUNK), 1))
    for g in range(SG_GROUPS):
        wg = jnp.where(tri, sgw_ref[g], 0.0).astype(BF16)
        cols = slice(g * SG_GROUP_DIM, (g + 1) * SG_GROUP_DIM)
        for n in range(tm // SG_CHUNK):
            rows = slice(n * SG_CHUNK, (n + 1) * SG_CHUNK)
            mix = _dot(wg, v_ref[rows, cols]) + sgb_ref[:, cols]
            yb_s[rows, cols] = (u_ref[rows, cols].astype(F32) * mix).astype(BF16)

    da = _dot(ya_ref[...], wdn_ref[...])
    db = _dot(yb_s[...], wsg_ref[...])
    merged = ga_ref[...].astype(F32) * da + gb_ref[...].astype(F32) * db
    x1 = x_ref[...] + _dot(merged.astype(BF16), wout_ref[...])
    x1_ref[...] = x1
    ms = jnp.mean(x1 * x1, axis=-1, keepdims=True)
    h2 = x1 * lax.rsqrt(ms + RMS_EPS) * g2_ref[...]
    h2_ref[...] = _pack_bf16_pairs(h2)

    logits = _dot_nt(wrt_ref[...], h2.astype(BF16)) + br_ref[...]
    eidx = lax.broadcasted_iota(jnp.int32, (ne, tm), 0).astype(F32)
    work = logits
    onehot = jnp.zeros((ne, tm), F32)
    sels, tops, idxs = [], [], []
    for _ in range(TOP_K):
        m = jnp.max(work, axis=0, keepdims=True)
        idx = jnp.min(jnp.where(work == m, eidx, float(ne)), axis=0, keepdims=True)
        sel = eidx == idx
        work = jnp.where(sel, -jnp.inf, work)
        onehot = onehot + sel.astype(F32)
        sels.append(sel)
        tops.append(m)
        idxs.append(idx)
    ps = [jnp.exp(m - tops[0]) for m in tops]
    den = ps[0] + ps[1] + ps[2] + ps[3]
    before = (lax.broadcasted_iota(jnp.int32, (tm, tm), 0)
              < lax.broadcasted_iota(jnp.int32, (tm, tm), 1)).astype(BF16)
    cum = _dot(onehot.astype(BF16), before) + carry_s[:, 0:1]
    ranks = [jnp.sum(jnp.where(sel, cum, 0.0), axis=0, keepdims=True) for sel in sels]
    route_ref[...] = jnp.concatenate(idxs + ranks, axis=0).astype(jnp.int32)
    rw_ref[...] = jnp.concatenate([p / den for p in ps] + [jnp.zeros((TOP_K, tm), F32)], axis=0)
    carry_s[...] = carry_s[...] + jnp.sum(onehot, axis=1, keepdims=True)
    cnt_ref[...] = carry_s[...]


def _merge_route(x2, ya, u, v, ga, gb, sg_w, sg_b, w_o_dn, w_o_sg, w_out, norm2_g, w_router, b_router):
    t, d = x2.shape
    tm = ROW_TILE
    ne = w_router.shape[1]
    sgb_full = jnp.repeat(sg_b.T.astype(F32), SG_GROUP_DIM, axis=1)
    row = lambda w: pl.BlockSpec((tm, w), lambda i: (i, 0))
    col = lambda h: pl.BlockSpec((h, tm), lambda i: (0, i))
    out_shape = (
        jax.ShapeDtypeStruct((t, d), F32),
        jax.ShapeDtypeStruct((t, d // 2), jnp.int32),
        jax.ShapeDtypeStruct((2 * TOP_K, t), jnp.int32),
        jax.ShapeDtypeStruct((2 * TOP_K, t), F32),
        jax.ShapeDtypeStruct((ne, LANES), F32),
    )
    return pl.pallas_call(
        _merge_kernel,
        out_shape=out_shape,
        grid=(t // tm,),
        in_specs=[
            row(d), row(DN_WIDTH), row(SG_WIDTH), row(SG_WIDTH), row(d), row(d),
            _const_spec(sg_w.shape), _const_spec(sgb_full.shape),
            _const_spec(w_o_dn.shape), _const_spec(w_o_sg.shape), _const_spec(w_out.shape),
            _const_spec((1, d)), _const_spec((ne, d)), _const_spec((ne, 1)),
        ],
        out_specs=(row(d), row(d // 2), col(2 * TOP_K), col(2 * TOP_K), _const_spec((ne, LANES))),
        scratch_shapes=[pltpu.VMEM((tm, SG_WIDTH), BF16), pltpu.VMEM((ne, LANES), F32)],
        compiler_params=_params(("arbitrary",)),
        name="merge_route",
    )(x2, ya, u, v, ga, gb, sg_w.astype(F32), sgb_full, w_o_dn.astype(BF16), w_o_sg.astype(BF16),
      w_out.astype(BF16), norm2_g.reshape(1, d).astype(F32), w_router.T.astype(BF16),
      b_router.reshape(ne, 1).astype(F32))


def _sc_mesh():
    return plsc.VectorSubcoreMesh(core_axis_name="c", subcore_axis_name="s")


def _dispatch(h2, dest3, n_rows):
    n_tok, d = h2.shape
    chunk = dest3.shape[2]
    per_w = n_tok // SC_WORKERS
    n_chunks = per_w // chunk

    @functools.partial(
        pl.kernel, mesh=_sc_mesh(),
        out_type=jax.ShapeDtypeStruct((n_rows, d), h2.dtype),
        scratch_types=[pltpu.VMEM((TOP_K, chunk), jnp.int32), pltpu.VMEM((chunk, d), h2.dtype)],
        name="moe_dispatch",
    )
    def k(h2_hbm, dest_hbm, xs_hbm, idx_v, rows_v):
        wid = lax.axis_index("s") * SC_CORES + lax.axis_index("c")
        base = wid * per_w

        @pl.loop(0, n_chunks)
        def _(j):
            t0 = pl.multiple_of(base + j * chunk, chunk)
            pltpu.sync_copy(dest_hbm.at[base // chunk + j], idx_v)
            pltpu.sync_copy(h2_hbm.at[pl.ds(t0, chunk)], rows_v)
            for kk in range(TOP_K):
                pltpu.sync_copy(rows_v, xs_hbm.at[idx_v.at[kk]])

    return k(h2, dest3)


def _collect(y, dest3):
    n_tok = dest3.shape[0] * dest3.shape[2]
    chunk = dest3.shape[2]
    d = y.shape[1]
    per_w = n_tok // SC_WORKERS
    n_chunks = per_w // chunk

    @functools.partial(
        pl.kernel, mesh=_sc_mesh(),
        out_type=jax.ShapeDtypeStruct((TOP_K, n_tok, d), y.dtype),
        scratch_types=[pltpu.VMEM((TOP_K, chunk), jnp.int32), pltpu.VMEM((chunk, d), y.dtype)],
        name="moe_collect",
    )
    def k(y_hbm, dest_hbm, yg_hbm, idx_v, rows_v):
        wid = lax.axis_index("s") * SC_CORES + lax.axis_index("c")
        base = wid * per_w

        @pl.loop(0, n_chunks)
        def _(j):
            t0 = pl.multiple_of(base + j * chunk, chunk)
            pltpu.sync_copy(dest_hbm.at[base // chunk + j], idx_v)
            for kk in range(TOP_K):
                pltpu.sync_copy(y_hbm.at[idx_v.at[kk]], rows_v)
                pltpu.sync_copy(rows_v, yg_hbm.at[kk, pl.ds(t0, chunk)])

    return k(y, dest3)


def _expert_kernel(be_ref, nv_ref, xs_ref, wgu_ref, bgu_ref, wd_ref, bd_ref, y_ref, wgu_s, wd_s):
    i = pl.program_id(0)
    dff = wd_ref.shape[1]

    @pl.when(jnp.logical_or(i == 0, be_ref[i] != be_ref[jnp.maximum(i - 1, 0)]))
    def _():
        wgu_s[...] = wgu_ref[0].astype(BF16)
        wd_s[...] = wd_ref[0].astype(BF16)

    @pl.when(i < nv_ref[0])
    def _():
        gu = _dot(_unpack_bf16_pairs(xs_ref[...]).astype(BF16), wgu_s[...]) + bgu_ref[0]
        x_glu = jnp.minimum(gu[:, :dff], SWIGLU_LIMIT)
        x_lin = jnp.clip(gu[:, dff:], -SWIGLU_LIMIT, SWIGLU_LIMIT)
        act = x_glu * _sigmoid(SWIGLU_ALPHA * x_glu) * (x_lin + 1.0)
        y_ref[...] = _pack_bf16_pairs(_dot(act.astype(BF16), wd_s[...]) + bd_ref[0])

    @pl.when(i >= nv_ref[0])
    def _():
        y_ref[...] = jnp.zeros_like(y_ref)


def _expert_ffn(xs, block_expert, n_valid, w_gate_up, b_gate_up, w_down, b_down):
    n_rows = xs.shape[0]
    tm = MOE_TILE
    ne, d, dff2 = w_gate_up.shape
    dff = dff2 // 2
    n_blocks = n_rows // tm
    grid_spec = pltpu.PrefetchScalarGridSpec(
        num_scalar_prefetch=2,
        grid=(n_blocks,),
        in_specs=[
            pl.BlockSpec((tm, d // 2), lambda i, be, nv: (jnp.minimum(i, nv[0] - 1), 0)),
            pl.BlockSpec((1, d, dff2), lambda i, be, nv: (be[i], 0, 0)),
            pl.BlockSpec((1, 1, dff2), lambda i, be, nv: (be[i], 0, 0)),
            pl.BlockSpec((1, dff, d), lambda i, be, nv: (be[i], 0, 0)),
            pl.BlockSpec((1, 1, d), lambda i, be, nv: (be[i], 0, 0)),
        ],
        out_specs=pl.BlockSpec((tm, d // 2), lambda i, be, nv: (i, 0)),
        scratch_shapes=[pltpu.VMEM((d, dff2), BF16), pltpu.VMEM((dff, d), BF16)],
    )
    return pl.pallas_call(
        _expert_kernel,
        out_shape=jax.ShapeDtypeStruct((n_rows, d // 2), jnp.int32),
        grid_spec=grid_spec,
        compiler_params=_params(("arbitrary",)),
        name="expert_ffn",
    )(block_expert, n_valid, xs, w_gate_up, b_gate_up.reshape(ne, 1, dff2).astype(F32),
      w_down, b_down.reshape(ne, 1, d).astype(F32))


def _final_kernel(x1_ref, yg_ref, rw_ref, gf_ref, o_ref):
    acc = x1_ref[...]
    rw = rw_ref[...]
    for kk in range(TOP_K):
        acc = acc + rw[:, kk:kk + 1] * _unpack_bf16_pairs(yg_ref[kk])
    ms = jnp.mean(acc * acc, axis=-1, keepdims=True)
    o_ref[...] = acc * lax.rsqrt(ms + RMS_EPS) * gf_ref[...]


def _combine_final(x1, yg, rw_tok, norm_f_g):
    t, d = x1.shape
    tm = ROW_TILE
    return pl.pallas_call(
        _final_kernel,
        out_shape=jax.ShapeDtypeStruct((t, d), F32),
        grid=(t // tm,),
        in_specs=[
            pl.BlockSpec((tm, d), lambda i: (i, 0)),
            pl.BlockSpec((TOP_K, tm, d // 2), lambda i: (0, i, 0)),
            pl.BlockSpec((tm, 2 * TOP_K), lambda i: (i, 0)),
            _const_spec((1, d)),
        ],
        out_specs=pl.BlockSpec((tm, d), lambda i: (i, 0)),
        compiler_params=_params(("parallel",)),
        name="combine_final",
    )(x1, yg, rw_tok, norm_f_g.reshape(1, d).astype(F32))


def _routing_tables(route, counts, n_tok):
    tm = MOE_TILE
    n_blocks = (n_tok * TOP_K) // tm + N_EXPERTS
    cnt = counts[:, 0].astype(jnp.int32)
    padded = (cnt + tm - 1) // tm * tm
    pad_end = jnp.cumsum(padded)
    pad_start = pad_end - padded
    eid = route[:TOP_K]
    rank = route[TOP_K:]
    start = jnp.sum(jnp.where(eid[..., None] == jnp.arange(N_EXPERTS, dtype=jnp.int32),
                              pad_start, 0), axis=-1)
    dest = (start + rank).astype(jnp.int32)
    dest3 = dest.reshape(TOP_K, n_tok // SC_CHUNK, SC_CHUNK).transpose(1, 0, 2)
    n_valid = (pad_end[-1] // tm).astype(jnp.int32)
    blk = jnp.arange(n_blocks, dtype=jnp.int32)
    first_row = jnp.minimum(blk, n_valid - 1) * tm
    be = jnp.sum((pad_end[None, :] <= first_row[:, None]).astype(jnp.int32), axis=1)
    be = jnp.minimum(be, N_EXPERTS - 1).astype(jnp.int32)
    return dest3, be, n_valid.reshape(1), n_blocks * tm


def _layer(x, norm1_g, w_in, conv_w, dn_a_log, dn_dt_bias, dn_norm_g, w_o_dn, sg_ln_g, sg_ln_b,
           sg_w, sg_b, w_o_sg, w_out, norm2_g, w_router, b_router, w_gate_up, b_gate_up, w_down, b_down):
    bsz, seq, d = x.shape
    t = bsz * seq
    x2 = x.reshape(t, d)
    qkv, zs, bg, bgt, u, v, ga, gb = _project(x2, norm1_g, w_in, dn_a_log, dn_dt_bias, sg_ln_g, sg_ln_b)
    ya = _gated_delta(qkv, bg, bgt, zs, conv_w, dn_norm_g, bsz, seq)
    x1, h2, route, rw, counts = _merge_route(x2, ya, u, v, ga, gb, sg_w, sg_b, w_o_dn, w_o_sg, w_out,
                                             norm2_g, w_router, b_router)
    dest3, block_expert, n_valid, n_rows = _routing_tables(route, counts, t)
    xs = _dispatch(h2, dest3, n_rows)
    y = _expert_ffn(xs, block_expert, n_valid, w_gate_up, b_gate_up, w_down, b_down)
    yg = _collect(y, dest3)
    return x1, yg, rw.T


def kernel(x, norm1_g, w_in, conv_w, dn_a_log, dn_dt_bias, dn_norm_g, w_o_dn, sg_ln_g, sg_ln_b, sg_w, sg_b, w_o_sg, w_out, norm2_g, w_router, b_router, w_gate_up, b_gate_up, w_down, b_down, norm_f_g):
    bsz, seq, d = x.shape
    depth = norm1_g.shape[0]
    ident = jnp.ones((d,), F32)
    for layer in range(depth):
        x1, yg, rw_tok = _layer(
            x, norm1_g[layer], w_in[layer], conv_w[layer], dn_a_log[layer], dn_dt_bias[layer],
            dn_norm_g[layer], w_o_dn[layer], sg_ln_g[layer], sg_ln_b[layer], sg_w[layer], sg_b[layer],
            w_o_sg[layer], w_out[layer], norm2_g[layer], w_router[layer], b_router[layer],
            w_gate_up[layer], b_gate_up[layer], w_down[layer], b_down[layer])
        last = layer == depth - 1
        if not last:
            raise NotImplementedError("stacked layers need an un-normalised combine")
        x = _combine_final(x1, yg, rw_tok, norm_f_g if last else ident).reshape(bsz, seq, d)
    return x
```

```python
import functools

import jax
import jax.numpy as jnp
from jax import lax
from jax.experimental import pallas as pl
from jax.experimental.pallas import tpu as pltpu
from jax.experimental.pallas import tpu_sc as plsc

F32 = jnp.float32
BF16 = jnp.bfloat16

DN_HEADS = 4
DN_HEAD_DIM = 128
DN_WIDTH = DN_HEADS * DN_HEAD_DIM
CONV_WIDTH = 4
SG_GROUPS = 4
SG_GROUP_DIM = 128
SG_WIDTH = SG_GROUPS * SG_GROUP_DIM
SG_CHUNK = 128
N_EXPERTS = 32
TOP_K = 4
SWIGLU_LIMIT = 7.0
SWIGLU_ALPHA = 1.702
RMS_EPS = 1e-6
LN_EPS = 1e-5
L2_EPS = 1e-6

OFF_Z = 3 * DN_WIDTH
OFF_BETA = OFF_Z + DN_WIDTH
OFF_U = OFF_BETA + 2 * DN_HEADS

LANES = 128
SUBLANES = 8
SC_CORES = 2
SC_SUBCORES = 16
SC_WORKERS = SC_CORES * SC_SUBCORES
VMEM_LIMIT = 56 * 1024 * 1024

ROW_TILE = 512
GDN_TILE = 512
GDN_CHUNK = 128
MOE_TILE = 512
SC_CHUNK = 64

_NT = (((1,), (1,)), ((), ()))
_TN = (((0,), (0,)), ((), ()))


def _dot(a, b):
    return jnp.dot(a, b, preferred_element_type=F32)


def _dot_nt(a, b):
    return lax.dot_general(a, b, _NT, preferred_element_type=F32)


def _dot_tn(a, b):
    return lax.dot_general(a, b, _TN, preferred_element_type=F32)


def _sigmoid(x):
    return 1.0 / (1.0 + jnp.exp(-x))


def _softplus(x):
    return jnp.maximum(x, 0.0) + jnp.log1p(jnp.exp(-jnp.abs(x)))


def _split3(x):
    hi = x.astype(BF16)
    r = x - hi.astype(F32)
    mid = r.astype(BF16)
    lo = (r - mid.astype(F32)).astype(BF16)
    return hi, mid, lo


def _pack_bf16_pairs(x):
    n = x.shape[1] // 2
    lo = lax.bitcast_convert_type(x[:, :n].astype(BF16).astype(F32), jnp.int32)
    hi = lax.bitcast_convert_type(x[:, n:].astype(BF16).astype(F32), jnp.int32)
    return lax.shift_right_logical(lo, 16) | (hi & jnp.int32(-65536))


def _unpack_bf16_pairs(p):
    lo = lax.bitcast_convert_type(lax.shift_left(p, 16), F32)
    hi = lax.bitcast_convert_type(p & jnp.int32(-65536), F32)
    return jnp.concatenate([lo, hi], axis=1)


def _const_spec(shape):
    nd = len(shape)
    return pl.BlockSpec(shape, lambda *_: (0,) * nd)


def _params(sem):
    return pltpu.CompilerParams(dimension_semantics=sem, vmem_limit_bytes=VMEM_LIMIT)


def _proj_kernel(x_ref, g1_ref, wqkv_ref, wz_ref, wba_ref, wbat_ref, wuv_ref, wga_ref, wgb_ref,
                 alog_r_ref, dtb_r_ref, alog_c_ref, dtb_c_ref, lng_ref, lnb_ref,
                 qkv_ref, zs_ref, bg_ref, bgt_ref, u_ref, v_ref, ga_ref, gb_ref):
    x = x_ref[...]
    ms = jnp.mean(x * x, axis=-1, keepdims=True)
    h = (x * lax.rsqrt(ms + RMS_EPS) * g1_ref[...]).astype(BF16)

    qkv_ref[...] = _dot(h, wqkv_ref[...])

    z = _dot(h, wz_ref[...])
    zs_ref[...] = (z * _sigmoid(z)).astype(BF16)

    ba = _dot(h, wba_ref[...])
    lane = lax.broadcasted_iota(jnp.int32, ba.shape, 1)
    g_col = -jnp.exp(alog_r_ref[...]) * _softplus(ba + dtb_r_ref[...])
    bg_ref[...] = jnp.where(lane < DN_HEADS, _sigmoid(ba), g_col)
    bat = _dot_nt(wbat_ref[...], h)
    row = lax.broadcasted_iota(jnp.int32, bat.shape, 0)
    g_row = -jnp.exp(alog_c_ref[...]) * _softplus(bat + dtb_c_ref[...])
    bgt_ref[...] = jnp.where(row < DN_HEADS, _sigmoid(bat), g_row)

    uv = _dot(h, wuv_ref[...])
    ge = 0.5 * uv * (1.0 + lax.erf(uv * (0.5 ** 0.5)))
    u_ref[...] = ge[:, :SG_WIDTH].astype(BF16)
    v = ge[:, SG_WIDTH:]
    mu = jnp.mean(v, axis=-1, keepdims=True)
    vc = v - mu
    var = jnp.mean(vc * vc, axis=-1, keepdims=True)
    v_ref[...] = (vc * lax.rsqrt(var + LN_EPS) * lng_ref[...] + lnb_ref[...]).astype(BF16)

    ga_ref[...] = _sigmoid(_dot(h, wga_ref[...])).astype(BF16)
    gb_ref[...] = _sigmoid(_dot(h, wgb_ref[...])).astype(BF16)


def _project(x2, norm1_g, w_in, dn_a_log, dn_dt_bias, sg_ln_g, sg_ln_b):
    t, d = x2.shape
    tm = ROW_TILE
    wb = w_in.astype(BF16)
    wqkv = wb[:, :OFF_Z]
    wz = wb[:, OFF_Z:OFF_BETA]
    wba8 = wb[:, OFF_BETA:OFF_U]
    wba = jnp.pad(wba8, ((0, 0), (0, LANES - 2 * DN_HEADS)))
    wbat = wba8.T
    wuv = wb[:, OFF_U:OFF_U + 2 * SG_WIDTH]
    wga = wb[:, OFF_U + 2 * SG_WIDTH:OFF_U + 2 * SG_WIDTH + d]
    wgb = wb[:, OFF_U + 2 * SG_WIDTH + d:]
    zeros4 = jnp.zeros((DN_HEADS,), F32)
    alog8 = jnp.concatenate([zeros4, dn_a_log.astype(F32)])
    dtb8 = jnp.concatenate([zeros4, dn_dt_bias.astype(F32)])
    alog_r = jnp.pad(alog8, (0, LANES - 2 * DN_HEADS)).reshape(1, LANES)
    dtb_r = jnp.pad(dtb8, (0, LANES - 2 * DN_HEADS)).reshape(1, LANES)
    alog_c = alog8.reshape(2 * DN_HEADS, 1)
    dtb_c = dtb8.reshape(2 * DN_HEADS, 1)

    row = lambda w: pl.BlockSpec((tm, w), lambda i: (i, 0))
    out_shape = (
        jax.ShapeDtypeStruct((t, OFF_Z), F32),
        jax.ShapeDtypeStruct((t, DN_WIDTH), BF16),
        jax.ShapeDtypeStruct((t, LANES), F32),
        jax.ShapeDtypeStruct((2 * DN_HEADS, t), F32),
        jax.ShapeDtypeStruct((t, SG_WIDTH), BF16),
        jax.ShapeDtypeStruct((t, SG_WIDTH), BF16),
        jax.ShapeDtypeStruct((t, d), BF16),
        jax.ShapeDtypeStruct((t, d), BF16),
    )
    return pl.pallas_call(
        _proj_kernel,
        out_shape=out_shape,
        grid=(t // tm,),
        in_specs=[
            row(d), _const_spec((1, d)),
            _const_spec(wqkv.shape), _const_spec(wz.shape), _const_spec(wba.shape),
            _const_spec(wbat.shape), _const_spec(wuv.shape), _const_spec(wga.shape),
            _const_spec(wgb.shape),
            _const_spec((1, LANES)), _const_spec((1, LANES)),
            _const_spec((2 * DN_HEADS, 1)), _const_spec((2 * DN_HEADS, 1)),
            _const_spec((1, SG_WIDTH)), _const_spec((1, SG_WIDTH)),
        ],
        out_specs=(
            row(OFF_Z), row(DN_WIDTH), row(LANES),
            pl.BlockSpec((2 * DN_HEADS, tm), lambda i: (0, i)),
            row(SG_WIDTH), row(SG_WIDTH), row(d), row(d),
        ),
        compiler_params=_params(("parallel",)),
        name="in_proj",
    )(x2, norm1_g.reshape(1, d), wqkv, wz, wba, wbat, wuv, wga, wgb,
      alog_r, dtb_r, alog_c, dtb_c, sg_ln_g.reshape(1, SG_WIDTH), sg_ln_b.reshape(1, SG_WIDTH))


_M_DIAG8, _M_OFF16, _M_OFF32, _M_OFF64, _M_OFF128, _M_INCL, _M_STRICT, _M_EYE, _M_UPPER = range(9)


def _gdn_constants(c):
    r = jnp.arange(c)[:, None]
    s = jnp.arange(c)[None, :]
    same = lambda b: (r // b) == (s // b)
    mats = [same(8)]
    b = 8
    while b < c:
        mats.append(same(2 * b) & ~same(b) & (r > s))
        b *= 2
    mats += [r >= s, r > s, r == s, r <= s]
    return jnp.stack(mats).astype(F32)


def _unit_lower_inverse(a_list, cm_ref):
    eye = cm_ref[_M_EYE]
    a0 = [(a * cm_ref[_M_DIAG8]).astype(BF16) for a in a_list]
    a2 = [_dot(x, x).astype(BF16) for x in a0]
    t = [eye - x.astype(F32) for x in a0]
    t = [ti + _dot(ti.astype(BF16), x) for ti, x in zip(t, a2)]
    a4 = [_dot(x, x).astype(BF16) for x in a2]
    t = [ti + _dot(ti.astype(BF16), x) for ti, x in zip(t, a4)]
    n_levels = cm_ref.shape[0] - 5
    for lvl in range(n_levels):
        off = [(a * cm_ref[_M_OFF16 + lvl]).astype(BF16) for a in a_list]
        tb = [ti.astype(BF16) for ti in t]
        x = [_dot(tbi, o).astype(BF16) for tbi, o in zip(tb, off)]
        t = [ti - _dot(xi, tbi) for ti, xi, tbi in zip(t, x, tb)]
    return t


def _gdn_kernel(qkv_ref, bg_ref, bgt_ref, zs_ref, cw_ref, gn_ref, cm_ref, ya_ref,
                xbuf, s_ref, q_s, k_s, v_s):
    c = pl.program_id(1)
    lt = qkv_ref.shape[0]
    ch = cm_ref.shape[1]
    hd = DN_HEAD_DIM
    halo = SUBLANES

    @pl.when(c == 0)
    def _():
        s_ref[...] = jnp.zeros_like(s_ref)
        xbuf[0:halo, :] = jnp.zeros((halo, xbuf.shape[1]), F32)

    @pl.when(c > 0)
    def _():
        xbuf[0:halo, :] = xbuf[lt:lt + halo, :]

    xbuf[halo:lt + halo, :] = qkv_ref[...]
    cw = cw_ref[...]
    conv = cw[CONV_WIDTH - 1:CONV_WIDTH, :] * xbuf[halo:lt + halo, :]
    for kk in range(CONV_WIDTH - 1):
        conv = conv + cw[kk:kk + 1, :] * xbuf[pl.ds(halo - (CONV_WIDTH - 1) + kk, lt), :]
    act = conv * _sigmoid(conv)
    for h in range(DN_HEADS):
        qh = act[:, h * hd:(h + 1) * hd]
        kh = act[:, DN_WIDTH + h * hd:DN_WIDTH + (h + 1) * hd]
        q_s[h] = qh * (lax.rsqrt(jnp.sum(qh * qh, axis=-1, keepdims=True) + L2_EPS) * (hd ** -0.5))
        k_s[h] = kh * lax.rsqrt(jnp.sum(kh * kh, axis=-1, keepdims=True) + L2_EPS)
        v_s[h] = act[:, 2 * DN_WIDTH + h * hd:2 * DN_WIDTH + (h + 1) * hd]

    nc = lt // ch
    lower16 = cm_ref[_M_INCL].astype(BF16)
    upper16 = cm_ref[_M_UPPER].astype(BF16)
    incl = cm_ref[_M_INCL] > 0.0

    bgc, gam, e_gam, e_rest, e_last, gam_t = [], [], [], [], [], []
    for ci in range(nc):
        rows = slice(ci * ch, (ci + 1) * ch)
        b = bg_ref[rows, :]
        hi, mid, lo = _split3(b)
        g = _dot(lower16, hi) + _dot(lower16, mid) + _dot(lower16, lo)
        g_last = g[ch - 1:ch, :]
        bgc.append(b)
        gam.append(g)
        e_gam.append(jnp.exp(g))
        e_rest.append(jnp.exp(g_last - g))
        e_last.append(jnp.exp(g_last))
        hi, mid, lo = _split3(bgt_ref[:, rows])
        gam_t.append(_dot(hi, upper16) + _dot(mid, upper16) + _dot(lo, upper16))

    items = [(ci, h) for ci in range(nc) for h in range(DN_HEADS)]
    a_list, qk16, vb16, kbe16, qe16, kd16 = [], [], [], [], [], []
    for ci, h in items:
        rows = slice(ci * ch, (ci + 1) * ch)
        gl = DN_HEADS + h
        beta = bgc[ci][:, h:h + 1]
        decay = jnp.exp(jnp.where(incl, gam[ci][:, gl:gl + 1] - gam_t[ci][gl:gl + 1, :], -jnp.inf))
        q = q_s[h, rows, :]
        k = k_s[h, rows, :]
        kb = k * beta
        k16 = k.astype(BF16)
        eg = e_gam[ci][:, gl:gl + 1]
        a_list.append(_dot_nt(kb.astype(BF16), k16) * decay * cm_ref[_M_STRICT])
        qk16.append((_dot_nt(q.astype(BF16), k16) * decay).astype(BF16))
        vb16.append((v_s[h, rows, :] * beta).astype(BF16))
        kbe16.append((kb * eg).astype(BF16))
        qe16.append((q * eg).astype(BF16))
        kd16.append((k * e_rest[ci][:, gl:gl + 1]).astype(BF16))
    t16 = [t.astype(BF16) for t in _unit_lower_inverse(a_list, cm_ref)]
    u_list = [_dot(t, x) for t, x in zip(t16, vb16)]
    w16 = [_dot(t, x).astype(BF16) for t, x in zip(t16, kbe16)]

    gn = gn_ref[...]
    state = [s_ref[h] for h in range(DN_HEADS)]
    for ci in range(nc):
        rows = slice(ci * ch, (ci + 1) * ch)
        idx = [ci * DN_HEADS + h for h in range(DN_HEADS)]
        st16 = [st.astype(BF16) for st in state]
        vn16 = [(u_list[i] - _dot(w16[i], st)).astype(BF16) for i, st in zip(idx, st16)]
        o = [_dot(qe16[i], st) + _dot(qk16[i], vn) for i, st, vn in zip(idx, st16, vn16)]
        state = [st * e_last[ci][:, DN_HEADS + h:DN_HEADS + h + 1] + _dot_tn(kd16[i], vn)
                 for h, (i, st, vn) in enumerate(zip(idx, state, vn16))]
        for h in range(DN_HEADS):
            oh = o[h]
            on = oh * lax.rsqrt(jnp.mean(oh * oh, axis=-1, keepdims=True) + RMS_EPS) * gn
            zs = zs_ref[rows, h * hd:(h + 1) * hd].astype(F32)
            ya_ref[rows, h * hd:(h + 1) * hd] = (on * zs).astype(BF16)
    for h in range(DN_HEADS):
        s_ref[h] = state[h]


def _gated_delta(qkv, bg, bgt, zs, conv_w, dn_norm_g, batch, seq):
    t = batch * seq
    lt = GDN_TILE
    nt = seq // lt
    cm = _gdn_constants(GDN_CHUNK)
    w3 = qkv.shape[1]
    row = lambda w: pl.BlockSpec((lt, w), lambda b, c: (b * nt + c, 0))
    return pl.pallas_call(
        _gdn_kernel,
        out_shape=jax.ShapeDtypeStruct((t, DN_WIDTH), BF16),
        grid=(batch, nt),
        in_specs=[
            row(w3), row(LANES),
            pl.BlockSpec((2 * DN_HEADS, lt), lambda b, c: (0, b * nt + c)),
            row(DN_WIDTH),
            _const_spec(conv_w.shape), _const_spec((1, DN_HEAD_DIM)), _const_spec(cm.shape),
        ],
        out_specs=row(DN_WIDTH),
        scratch_shapes=[
            pltpu.VMEM((lt + SUBLANES, w3), F32),
            pltpu.VMEM((DN_HEADS, DN_HEAD_DIM, DN_HEAD_DIM), F32),
            pltpu.VMEM((DN_HEADS, lt, DN_HEAD_DIM), F32),
            pltpu.VMEM((DN_HEADS, lt, DN_HEAD_DIM), F32),
            pltpu.VMEM((DN_HEADS, lt, DN_HEAD_DIM), F32),
        ],
        compiler_params=_params(("parallel", "arbitrary")),
        name="gated_delta",
    )(qkv, bg, bgt, zs, conv_w.astype(F32), dn_norm_g.reshape(1, DN_HEAD_DIM).astype(F32), cm)


def _merge_kernel(x_ref, ya_ref, u_ref, v_ref, ga_ref, gb_ref, sgw_ref, sgb_ref, wdn_ref, wsg_ref,
                  wout_ref, g2_ref, wrt_ref, br_ref,
                  x1_ref, h2_ref, route_ref, rw_ref, cnt_ref,
                  yb_s, carry_s):
    i = pl.program_id(0)
    tm = x_ref.shape[0]
    ne = wrt_ref.shape[0]

    @pl.when(i == 0)
    def _():
        carry_s[...] = jnp.zeros_like(carry_s)

    tri = (lax.broadcasted_iota(jnp.int32, (SG_CHUNK, SG_CHUNK), 0)
           >= lax.broadcasted_iota(jnp.int32, (SG_CHUNK, SG_CHUNK), 1))
    for g in range(SG_GROUPS):
        wg = jnp.where(tri, sgw_ref[g], 0.0).astype(BF16)
        cols = slice(g * SG_GROUP_DIM, (g + 1) * SG_GROUP_DIM)
        for n in range(tm // SG_CHUNK):
            rows = slice(n * SG_CHUNK, (n + 1) * SG_CHUNK)
            mix = _dot(wg, v_ref[rows, cols]) + sgb_ref[:, cols]
            yb_s[rows, cols] = (u_ref[rows, cols].astype(F32) * mix).astype(BF16)

    da = _dot(ya_ref[...], wdn_ref[...])
    db = _dot(yb_s[...], wsg_ref[...])
    merged = ga_ref[...].astype(F32) * da + gb_ref[...].astype(F32) * db
    x1 = x_ref[...] + _dot(merged.astype(BF16), wout_ref[...])
    x1_ref[...] = x1
    ms = jnp.mean(x1 * x1, axis=-1, keepdims=True)
    h2 = x1 * lax.rsqrt(ms + RMS_EPS) * g2_ref[...]
    h2_ref[...] = _pack_bf16_pairs(h2)

    logits = _dot_nt(wrt_ref[...], h2.astype(BF16)) + br_ref[...]
    eidx = lax.broadcasted_iota(jnp.int32, (ne, tm), 0).astype(F32)
    work = logits
    onehot = jnp.zeros((ne, tm), F32)
    sels, tops, idxs = [], [], []
    for _ in range(TOP_K):
        m = jnp.max(work, axis=0, keepdims=True)
        idx = jnp.min(jnp.where(work == m, eidx, float(ne)), axis=0, keepdims=True)
        sel = eidx == idx
        work = jnp.where(sel, -jnp.inf, work)
        onehot = onehot + sel.astype(F32)
        sels.append(sel)
        tops.append(m)
        idxs.append(idx)
    ps = [jnp.exp(m - tops[0]) for m in tops]
    den = ps[0] + ps[1] + ps[2] + ps[3]
    before = (lax.broadcasted_iota(jnp.int32, (tm, tm), 0)
              < lax.broadcasted_iota(jnp.int32, (tm, tm), 1)).astype(BF16)
    cum = _dot(onehot.astype(BF16), before) + carry_s[:, 0:1]
    ranks = [jnp.sum(jnp.where(sel, cum, 0.0), axis=0, keepdims=True) for sel in sels]
    route_ref[...] = jnp.concatenate(idxs + ranks, axis=0).astype(jnp.int32)
    rw_ref[...] = jnp.concatenate([p / den for p in ps] + [jnp.zeros((TOP_K, tm), F32)], axis=0)
    carry_s[...] = carry_s[...] + jnp.sum(onehot, axis=1, keepdims=True)
    cnt_ref[...] = carry_s[...]


def _merge_route(x2, ya, u, v, ga, gb, sg_w, sg_b, w_o_dn, w_o_sg, w_out, norm2_g, w_router, b_router):
    t, d = x2.shape
    tm = ROW_TILE
    ne = w_router.shape[1]
    sgb_full = jnp.repeat(sg_b.T.astype(F32), SG_GROUP_DIM, axis=1)
    row = lambda w: pl.BlockSpec((tm, w), lambda i: (i, 0))
    col = lambda h: pl.BlockSpec((h, tm), lambda i: (0, i))
    out_shape = (
        jax.ShapeDtypeStruct((t, d), F32),
        jax.ShapeDtypeStruct((t, d // 2), jnp.int32),
        jax.ShapeDtypeStruct((2 * TOP_K, t), jnp.int32),
        jax.ShapeDtypeStruct((2 * TOP_K, t), F32),
        jax.ShapeDtypeStruct((ne, LANES), F32),
    )
    return pl.pallas_call(
        _merge_kernel,
        out_shape=out_shape,
        grid=(t // tm,),
        in_specs=[
            row(d), row(DN_WIDTH), row(SG_WIDTH), row(SG_WIDTH), row(d), row(d),
            _const_spec(sg_w.shape), _const_spec(sgb_full.shape),
            _const_spec(w_o_dn.shape), _const_spec(w_o_sg.shape), _const_spec(w_out.shape),
            _const_spec((1, d)), _const_spec((ne, d)), _const_spec((ne, 1)),
        ],
        out_specs=(row(d), row(d // 2), col(2 * TOP_K), col(2 * TOP_K), _const_spec((ne, LANES))),
        scratch_shapes=[pltpu.VMEM((tm, SG_WIDTH), BF16), pltpu.VMEM((ne, LANES), F32)],
        compiler_params=_params(("arbitrary",)),
        name="merge_route",
    )(x2, ya, u, v, ga, gb, sg_w.astype(F32), sgb_full, w_o_dn.astype(BF16), w_o_sg.astype(BF16),
      w_out.astype(BF16), norm2_g.reshape(1, d).astype(F32), w_router.T.astype(BF16),
      b_router.reshape(ne, 1).astype(F32))


def _sc_mesh():
    return plsc.VectorSubcoreMesh(core_axis_name="c", subcore_axis_name="s")


def _dispatch(h2, dest3, n_rows):
    n_tok, d = h2.shape
    chunk = dest3.shape[2]
    per_w = n_tok // SC_WORKERS
    n_chunks = per_w // chunk
    nbuf = 2

    @functools.partial(
        pl.kernel, mesh=_sc_mesh(),
        out_type=jax.ShapeDtypeStruct((n_rows, d), h2.dtype),
        scratch_types=[
            pltpu.VMEM((n_chunks, TOP_K, chunk), jnp.int32),
            pltpu.VMEM((nbuf, chunk, d), h2.dtype),
            pltpu.SemaphoreType.DMA((nbuf,)),
            pltpu.SemaphoreType.DMA((nbuf,)),
        ],
        name="moe_dispatch",
    )
    def k(h2_hbm, dest_hbm, xs_hbm, idx_v, rows_v, sem_in, sem_out):
        wid = lax.axis_index("s") * SC_CORES + lax.axis_index("c")
        base = wid * per_w
        pltpu.sync_copy(dest_hbm.at[pl.ds(wid * n_chunks, n_chunks)], idx_v)
        loads = [None] * n_chunks
        stores = [None] * n_chunks
        for j in range(n_chunks + 1):
            if j < n_chunks:
                b = j % nbuf
                if j >= nbuf:
                    for cp in stores[j - nbuf]:
                        cp.wait()
                t0 = pl.multiple_of(base + j * chunk, chunk)
                loads[j] = pltpu.async_copy(h2_hbm.at[pl.ds(t0, chunk)], rows_v.at[b], sem_in.at[b])
            if j >= 1:
                pb = (j - 1) % nbuf
                loads[j - 1].wait()
                stores[j - 1] = [
                    pltpu.async_copy(rows_v.at[pb], xs_hbm.at[idx_v.at[j - 1, kk]], sem_out.at[pb])
                    for kk in range(TOP_K)]
        for j in range(max(n_chunks - nbuf, 0), n_chunks):
            for cp in stores[j]:
                cp.wait()

    return k(h2, dest3)


def _collect(y, dest3):
    n_tok = dest3.shape[0] * dest3.shape[2]
    chunk = dest3.shape[2]
    d = y.shape[1]
    per_w = n_tok // SC_WORKERS
    n_chunks = per_w // chunk
    n_items = n_chunks * TOP_K
    nbuf = 3

    @functools.partial(
        pl.kernel, mesh=_sc_mesh(),
        out_type=jax.ShapeDtypeStruct((TOP_K, n_tok, d), y.dtype),
        scratch_types=[
            pltpu.VMEM((n_chunks, TOP_K, chunk), jnp.int32),
            pltpu.VMEM((nbuf, chunk, d), y.dtype),
            pltpu.SemaphoreType.DMA((nbuf,)),
            pltpu.SemaphoreType.DMA((nbuf,)),
        ],
        name="moe_collect",
    )
    def k(y_hbm, dest_hbm, yg_hbm, idx_v, rows_v, sem_in, sem_out):
        wid = lax.axis_index("s") * SC_CORES + lax.axis_index("c")
        base = wid * per_w
        pltpu.sync_copy(dest_hbm.at[pl.ds(wid * n_chunks, n_chunks)], idx_v)
        gathers = [None] * n_items
        stores = [None] * n_items
        for m in range(n_items + 1):
            if m < n_items:
                b = m % nbuf
                if m >= nbuf:
                    stores[m - nbuf].wait()
                j, kk = divmod(m, TOP_K)
                gathers[m] = pltpu.async_copy(y_hbm.at[idx_v.at[j, kk]], rows_v.at[b], sem_in.at[b])
            if m >= 1:
                pb = (m - 1) % nbuf
                j, kk = divmod(m - 1, TOP_K)
                t0 = pl.multiple_of(base + j * chunk, chunk)
                gathers[m - 1].wait()
                stores[m - 1] = pltpu.async_copy(rows_v.at[pb], yg_hbm.at[kk, pl.ds(t0, chunk)], sem_out.at[pb])
        for m in range(max(n_items - nbuf, 0), n_items):
            stores[m].wait()

    return k(y, dest3)


def _expert_kernel(be_ref, nv_ref, xs_ref, wgu_ref, bgu_ref, wd_ref, bd_ref, y_ref, wgu_s, wd_s):
    i = pl.program_id(0)
    dff = wd_ref.shape[1]

    @pl.when(jnp.logical_or(i == 0, be_ref[i] != be_ref[jnp.maximum(i - 1, 0)]))
    def _():
        wgu_s[...] = wgu_ref[0].astype(BF16)
        wd_s[...] = wd_ref[0].astype(BF16)

    @pl.when(i < nv_ref[0])
    def _():
        gu = _dot(_unpack_bf16_pairs(xs_ref[...]).astype(BF16), wgu_s[...]) + bgu_ref[0]
        x_glu = jnp.minimum(gu[:, :dff], SWIGLU_LIMIT)
        x_lin = jnp.clip(gu[:, dff:], -SWIGLU_LIMIT, SWIGLU_LIMIT)
        act = x_glu * _sigmoid(SWIGLU_ALPHA * x_glu) * (x_lin + 1.0)
        y_ref[...] = _pack_bf16_pairs(_dot(act.astype(BF16), wd_s[...]) + bd_ref[0])

    @pl.when(i >= nv_ref[0])
    def _():
        y_ref[...] = jnp.zeros_like(y_ref)


def _expert_ffn(xs, block_expert, n_valid, w_gate_up, b_gate_up, w_down, b_down):
    n_rows = xs.shape[0]
    tm = MOE_TILE
    ne, d, dff2 = w_gate_up.shape
    dff = dff2 // 2
    n_blocks = n_rows // tm
    grid_spec = pltpu.PrefetchScalarGridSpec(
        num_scalar_prefetch=2,
        grid=(n_blocks,),
        in_specs=[
            pl.BlockSpec((tm, d // 2), lambda i, be, nv: (jnp.minimum(i, nv[0] - 1), 0)),
            pl.BlockSpec((1, d, dff2), lambda i, be, nv: (be[i], 0, 0)),
            pl.BlockSpec((1, 1, dff2), lambda i, be, nv: (be[i], 0, 0)),
            pl.BlockSpec((1, dff, d), lambda i, be, nv: (be[i], 0, 0)),
            pl.BlockSpec((1, 1, d), lambda i, be, nv: (be[i], 0, 0)),
        ],
        out_specs=pl.BlockSpec((tm, d // 2), lambda i, be, nv: (i, 0)),
        scratch_shapes=[pltpu.VMEM((d, dff2), BF16), pltpu.VMEM((dff, d), BF16)],
    )
    return pl.pallas_call(
        _expert_kernel,
        out_shape=jax.ShapeDtypeStruct((n_rows, d // 2), jnp.int32),
        grid_spec=grid_spec,
        compiler_params=_params(("arbitrary",)),
        name="expert_ffn",
    )(block_expert, n_valid, xs, w_gate_up, b_gate_up.reshape(ne, 1, dff2).astype(F32),
      w_down, b_down.reshape(ne, 1, d).astype(F32))


def _final_kernel(x1_ref, yg_ref, rw_ref, gf_ref, o_ref):
    acc = x1_ref[...]
    rw = rw_ref[...]
    for kk in range(TOP_K):
        acc = acc + rw[:, kk:kk + 1] * _unpack_bf16_pairs(yg_ref[kk])
    ms = jnp.mean(acc * acc, axis=-1, keepdims=True)
    o_ref[...] = acc * lax.rsqrt(ms + RMS_EPS) * gf_ref[...]


def _combine_final(x1, yg, rw_tok, norm_f_g):
    t, d = x1.shape
    tm = ROW_TILE
    return pl.pallas_call(
        _final_kernel,
        out_shape=jax.ShapeDtypeStruct((t, d), F32),
        grid=(t // tm,),
        in_specs=[
            pl.BlockSpec((tm, d), lambda i: (i, 0)),
            pl.BlockSpec((TOP_K, tm, d // 2), lambda i: (0, i, 0)),
            pl.BlockSpec((tm, 2 * TOP_K), lambda i: (i, 0)),
            _const_spec((1, d)),
        ],
        out_specs=pl.BlockSpec((tm, d), lambda i: (i, 0)),
        compiler_params=_params(("parallel",)),
        name="combine_final",
    )(x1, yg, rw_tok, norm_f_g.reshape(1, d).astype(F32))


def _routing_tables(route, counts, n_tok):
    tm = MOE_TILE
    n_blocks = (n_tok * TOP_K) // tm + N_EXPERTS
    cnt = counts[:, 0].astype(jnp.int32)
    padded = (cnt + tm - 1) // tm * tm
    pad_end = jnp.cumsum(padded)
    pad_start = pad_end - padded
    eid = route[:TOP_K]
    rank = route[TOP_K:]
    start = jnp.sum(jnp.where(eid[..., None] == jnp.arange(N_EXPERTS, dtype=jnp.int32),
                              pad_start, 0), axis=-1)
    dest = (start + rank).astype(jnp.int32)
    dest3 = dest.reshape(TOP_K, n_tok // SC_CHUNK, SC_CHUNK).transpose(1, 0, 2)
    n_valid = (pad_end[-1] // tm).astype(jnp.int32)
    blk = jnp.arange(n_blocks, dtype=jnp.int32)
    first_row = jnp.minimum(blk, n_valid - 1) * tm
    be = jnp.sum((pad_end[None, :] <= first_row[:, None]).astype(jnp.int32), axis=1)
    be = jnp.minimum(be, N_EXPERTS - 1).astype(jnp.int32)
    return dest3, be, n_valid.reshape(1), n_blocks * tm


def _layer(x, norm1_g, w_in, conv_w, dn_a_log, dn_dt_bias, dn_norm_g, w_o_dn, sg_ln_g, sg_ln_b,
           sg_w, sg_b, w_o_sg, w_out, norm2_g, w_router, b_router, w_gate_up, b_gate_up, w_down, b_down):
    bsz, seq, d = x.shape
    t = bsz * seq
    x2 = x.reshape(t, d)
    qkv, zs, bg, bgt, u, v, ga, gb = _project(x2, norm1_g, w_in, dn_a_log, dn_dt_bias, sg_ln_g, sg_ln_b)
    ya = _gated_delta(qkv, bg, bgt, zs, conv_w, dn_norm_g, bsz, seq)
    x1, h2, route, rw, counts = _merge_route(x2, ya, u, v, ga, gb, sg_w, sg_b, w_o_dn, w_o_sg, w_out,
                                             norm2_g, w_router, b_router)
    dest3, block_expert, n_valid, n_rows = _routing_tables(route, counts, t)
    xs = _dispatch(h2, dest3, n_rows)
    y = _expert_ffn(xs, block_expert, n_valid, w_gate_up, b_gate_up, w_down, b_down)
    yg = _collect(y, dest3)
    return x1, yg, rw.T


def kernel(x, norm1_g, w_in, conv_w, dn_a_log, dn_dt_bias, dn_norm_g, w_o_dn, sg_ln_g, sg_ln_b, sg_w, sg_b, w_o_sg, w_out, norm2_g, w_router, b_router, w_gate_up, b_gate_up, w_down, b_down, norm_f_g):
    bsz, seq, d = x.shape
    depth = norm1_g.shape[0]
    ident = jnp.ones((d,), F32)
    for layer in range(depth):
        x1, yg, rw_tok = _layer(
            x, norm1_g[layer], w_in[layer], conv_w[layer], dn_a_log[layer], dn_dt_bias[layer],
            dn_norm_g[layer], w_o_dn[layer], sg_ln_g[layer], sg_ln_b[layer], sg_w[layer], sg_b[layer],
            w_o_sg[layer], w_out[layer], norm2_g[layer], w_router[layer], b_router[layer],
            w_gate_up[layer], b_gate_up[layer], w_down[layer], b_down[layer])
        last = layer == depth - 1
        if not last:
            raise NotImplementedError("stacked layers need an un-normalised combine")
        x = _combine_final(x1, yg, rw_tok, norm_f_g if last else ident).reshape(bsz, seq, d)
    return x
```

```python
import functools

import jax
import jax.numpy as jnp
from jax import lax
from jax.experimental import pallas as pl
from jax.experimental.pallas import tpu as pltpu
from jax.experimental.pallas import tpu_sc as plsc

F32 = jnp.float32
BF16 = jnp.bfloat16

DN_HEADS = 4
DN_HEAD_DIM = 128
DN_WIDTH = DN_HEADS * DN_HEAD_DIM
CONV_WIDTH = 4
SG_GROUPS = 4
SG_GROUP_DIM = 128
SG_WIDTH = SG_GROUPS * SG_GROUP_DIM
SG_CHUNK = 128
N_EXPERTS = 32
TOP_K = 4
SWIGLU_LIMIT = 7.0
SWIGLU_ALPHA = 1.702
RMS_EPS = 1e-6
LN_EPS = 1e-5
L2_EPS = 1e-6

OFF_Z = 3 * DN_WIDTH
OFF_BETA = OFF_Z + DN_WIDTH
OFF_U = OFF_BETA + 2 * DN_HEADS

LANES = 128
SUBLANES = 8
SC_CORES = 2
SC_SUBCORES = 16
SC_WORKERS = SC_CORES * SC_SUBCORES
VMEM_LIMIT = 56 * 1024 * 1024

ROW_TILE = 512
GDN_TILE = 512
GDN_CHUNK = 128
MOE_TILE = 512
SC_CHUNK = 64

_NT = (((1,), (1,)), ((), ()))
_TN = (((0,), (0,)), ((), ()))


def _dot(a, b):
    return jnp.dot(a, b, preferred_element_type=F32)


def _dot_nt(a, b):
    return lax.dot_general(a, b, _NT, preferred_element_type=F32)


def _dot_tn(a, b):
    return lax.dot_general(a, b, _TN, preferred_element_type=F32)


def _sigmoid(x):
    return 1.0 / (1.0 + jnp.exp(-x))


def _softplus(x):
    return jnp.maximum(x, 0.0) + jnp.log1p(jnp.exp(-jnp.abs(x)))


def _split3(x):
    hi = x.astype(BF16)
    r = x - hi.astype(F32)
    mid = r.astype(BF16)
    lo = (r - mid.astype(F32)).astype(BF16)
    return hi, mid, lo


def _pack_bf16_pairs(x):
    n = x.shape[1] // 2
    lo = lax.bitcast_convert_type(x[:, :n].astype(BF16).astype(F32), jnp.int32)
    hi = lax.bitcast_convert_type(x[:, n:].astype(BF16).astype(F32), jnp.int32)
    return lax.shift_right_logical(lo, 16) | (hi & jnp.int32(-65536))


def _unpack_bf16_pairs(p):
    lo = lax.bitcast_convert_type(lax.shift_left(p, 16), F32)
    hi = lax.bitcast_convert_type(p & jnp.int32(-65536), F32)
    return jnp.concatenate([lo, hi], axis=1)


def _const_spec(shape):
    nd = len(shape)
    return pl.BlockSpec(shape, lambda *_: (0,) * nd)


def _params(sem):
    return pltpu.CompilerParams(dimension_semantics=sem, vmem_limit_bytes=VMEM_LIMIT)


def _proj_kernel(x_ref, g1_ref, wqkv_ref, wz_ref, wba_ref, wbat_ref, wuv_ref, wga_ref, wgb_ref,
                 alog_r_ref, dtb_r_ref, alog_c_ref, dtb_c_ref, lng_ref, lnb_ref,
                 qkv_ref, zs_ref, bg_ref, bgt_ref, u_ref, v_ref, ga_ref, gb_ref):
    x = x_ref[...]
    ms = jnp.mean(x * x, axis=-1, keepdims=True)
    h = (x * lax.rsqrt(ms + RMS_EPS) * g1_ref[...]).astype(BF16)

    qkv_ref[...] = _dot(h, wqkv_ref[...])

    z = _dot(h, wz_ref[...])
    zs_ref[...] = (z * _sigmoid(z)).astype(BF16)

    ba = _dot(h, wba_ref[...])
    lane = lax.broadcasted_iota(jnp.int32, ba.shape, 1)
    g_col = -jnp.exp(alog_r_ref[...]) * _softplus(ba + dtb_r_ref[...])
    bg_ref[...] = jnp.where(lane < DN_HEADS, _sigmoid(ba), g_col)
    bat = _dot_nt(wbat_ref[...], h)
    row = lax.broadcasted_iota(jnp.int32, bat.shape, 0)
    g_row = -jnp.exp(alog_c_ref[...]) * _softplus(bat + dtb_c_ref[...])
    bgt_ref[...] = jnp.where(row < DN_HEADS, _sigmoid(bat), g_row)

    uv = _dot(h, wuv_ref[...])
    ge = 0.5 * uv * (1.0 + lax.erf(uv * (0.5 ** 0.5)))
    u_ref[...] = ge[:, :SG_WIDTH].astype(BF16)
    v = ge[:, SG_WIDTH:]
    mu = jnp.mean(v, axis=-1, keepdims=True)
    vc = v - mu
    var = jnp.mean(vc * vc, axis=-1, keepdims=True)
    v_ref[...] = (vc * lax.rsqrt(var + LN_EPS) * lng_ref[...] + lnb_ref[...]).astype(BF16)

    ga_ref[...] = _sigmoid(_dot(h, wga_ref[...])).astype(BF16)
    gb_ref[...] = _sigmoid(_dot(h, wgb_ref[...])).astype(BF16)


def _project(x2, norm1_g, w_in, dn_a_log, dn_dt_bias, sg_ln_g, sg_ln_b):
    t, d = x2.shape
    tm = ROW_TILE
    wb = w_in.astype(BF16)
    wqkv = wb[:, :OFF_Z]
    wz = wb[:, OFF_Z:OFF_BETA]
    wba8 = wb[:, OFF_BETA:OFF_U]
    wba = jnp.pad(wba8, ((0, 0), (0, LANES - 2 * DN_HEADS)))
    wbat = wba8.T
    wuv = wb[:, OFF_U:OFF_U + 2 * SG_WIDTH]
    wga = wb[:, OFF_U + 2 * SG_WIDTH:OFF_U + 2 * SG_WIDTH + d]
    wgb = wb[:, OFF_U + 2 * SG_WIDTH + d:]
    zeros4 = jnp.zeros((DN_HEADS,), F32)
    alog8 = jnp.concatenate([zeros4, dn_a_log.astype(F32)])
    dtb8 = jnp.concatenate([zeros4, dn_dt_bias.astype(F32)])
    alog_r = jnp.pad(alog8, (0, LANES - 2 * DN_HEADS)).reshape(1, LANES)
    dtb_r = jnp.pad(dtb8, (0, LANES - 2 * DN_HEADS)).reshape(1, LANES)
    alog_c = alog8.reshape(2 * DN_HEADS, 1)
    dtb_c = dtb8.reshape(2 * DN_HEADS, 1)

    row = lambda w: pl.BlockSpec((tm, w), lambda i: (i, 0))
    out_shape = (
        jax.ShapeDtypeStruct((t, OFF_Z), F32),
        jax.ShapeDtypeStruct((t, DN_WIDTH), BF16),
        jax.ShapeDtypeStruct((t, LANES), F32),
        jax.ShapeDtypeStruct((2 * DN_HEADS, t), F32),
        jax.ShapeDtypeStruct((t, SG_WIDTH), BF16),
        jax.ShapeDtypeStruct((t, SG_WIDTH), BF16),
        jax.ShapeDtypeStruct((t, d), BF16),
        jax.ShapeDtypeStruct((t, d), BF16),
    )
    return pl.pallas_call(
        _proj_kernel,
        out_shape=out_shape,
        grid=(t // tm,),
        in_specs=[
            row(d), _const_spec((1, d)),
            _const_spec(wqkv.shape), _const_spec(wz.shape), _const_spec(wba.shape),
            _const_spec(wbat.shape), _const_spec(wuv.shape), _const_spec(wga.shape),
            _const_spec(wgb.shape),
            _const_spec((1, LANES)), _const_spec((1, LANES)),
            _const_spec((2 * DN_HEADS, 1)), _const_spec((2 * DN_HEADS, 1)),
            _const_spec((1, SG_WIDTH)), _const_spec((1, SG_WIDTH)),
        ],
        out_specs=(
            row(OFF_Z), row(DN_WIDTH), row(LANES),
            pl.BlockSpec((2 * DN_HEADS, tm), lambda i: (0, i)),
            row(SG_WIDTH), row(SG_WIDTH), row(d), row(d),
        ),
        compiler_params=_params(("parallel",)),
        name="in_proj",
    )(x2, norm1_g.reshape(1, d), wqkv, wz, wba, wbat, wuv, wga, wgb,
      alog_r, dtb_r, alog_c, dtb_c, sg_ln_g.reshape(1, SG_WIDTH), sg_ln_b.reshape(1, SG_WIDTH))


_M_DIAG8, _M_OFF16, _M_OFF32, _M_OFF64, _M_OFF128, _M_INCL, _M_STRICT, _M_EYE, _M_UPPER = range(9)


def _gdn_constants(c):
    r = jnp.arange(c)[:, None]
    s = jnp.arange(c)[None, :]
    same = lambda b: (r // b) == (s // b)
    mats = [same(8)]
    b = 8
    while b < c:
        mats.append(same(2 * b) & ~same(b) & (r > s))
        b *= 2
    mats += [r >= s, r > s, r == s, r <= s]
    return jnp.stack(mats).astype(F32)


def _unit_lower_inverse(a_list, cm_ref):
    eye = cm_ref[_M_EYE]
    a0 = [(a * cm_ref[_M_DIAG8]).astype(BF16) for a in a_list]
    a2 = [_dot(x, x).astype(BF16) for x in a0]
    t = [eye - x.astype(F32) for x in a0]
    t = [ti + _dot(ti.astype(BF16), x) for ti, x in zip(t, a2)]
    a4 = [_dot(x, x).astype(BF16) for x in a2]
    t = [ti + _dot(ti.astype(BF16), x) for ti, x in zip(t, a4)]
    n_levels = cm_ref.shape[0] - 5
    for lvl in range(n_levels):
        off = [(a * cm_ref[_M_OFF16 + lvl]).astype(BF16) for a in a_list]
        tb = [ti.astype(BF16) for ti in t]
        x = [_dot(tbi, o).astype(BF16) for tbi, o in zip(tb, off)]
        t = [ti - _dot(xi, tbi) for ti, xi, tbi in zip(t, x, tb)]
    return t


def _gdn_kernel(qkv_ref, bg_ref, bgt_ref, zs_ref, cw_ref, gn_ref, cm_ref, ya_ref,
                xbuf, s_ref, q_s, k_s, v_s):
    c = pl.program_id(1)
    lt = qkv_ref.shape[0]
    ch = cm_ref.shape[1]
    hd = DN_HEAD_DIM
    halo = SUBLANES

    @pl.when(c == 0)
    def _():
        s_ref[...] = jnp.zeros_like(s_ref)
        xbuf[0:halo, :] = jnp.zeros((halo, xbuf.shape[1]), F32)

    @pl.when(c > 0)
    def _():
        xbuf[0:halo, :] = xbuf[lt:lt + halo, :]

    xbuf[halo:lt + halo, :] = qkv_ref[...]
    cw = cw_ref[...]
    conv = cw[CONV_WIDTH - 1:CONV_WIDTH, :] * xbuf[halo:lt + halo, :]
    for kk in range(CONV_WIDTH - 1):
        conv = conv + cw[kk:kk + 1, :] * xbuf[pl.ds(halo - (CONV_WIDTH - 1) + kk, lt), :]
    act = conv * _sigmoid(conv)
    for h in range(DN_HEADS):
        qh = act[:, h * hd:(h + 1) * hd]
        kh = act[:, DN_WIDTH + h * hd:DN_WIDTH + (h + 1) * hd]
        q_s[h] = qh * (lax.rsqrt(jnp.sum(qh * qh, axis=-1, keepdims=True) + L2_EPS) * (hd ** -0.5))
        k_s[h] = kh * lax.rsqrt(jnp.sum(kh * kh, axis=-1, keepdims=True) + L2_EPS)
        v_s[h] = act[:, 2 * DN_WIDTH + h * hd:2 * DN_WIDTH + (h + 1) * hd]

    nc = lt // ch
    lower16 = cm_ref[_M_INCL].astype(BF16)
    upper16 = cm_ref[_M_UPPER].astype(BF16)
    incl = cm_ref[_M_INCL] > 0.0

    bgc, gam, e_gam, e_rest, e_last, gam_t = [], [], [], [], [], []
    for ci in range(nc):
        rows = slice(ci * ch, (ci + 1) * ch)
        b = bg_ref[rows, :]
        hi, mid, lo = _split3(b)
        g = _dot(lower16, hi) + _dot(lower16, mid) + _dot(lower16, lo)
        g_last = g[ch - 1:ch, :]
        bgc.append(b)
        gam.append(g)
        e_gam.append(jnp.exp(g))
        e_rest.append(jnp.exp(g_last - g))
        e_last.append(jnp.exp(g_last))
        hi, mid, lo = _split3(bgt_ref[:, rows])
        gam_t.append(_dot(hi, upper16) + _dot(mid, upper16) + _dot(lo, upper16))

    items = [(ci, h) for ci in range(nc) for h in range(DN_HEADS)]
    a_list, qk16, vb16, kbe16, qe16, kd16 = [], [], [], [], [], []
    for ci, h in items:
        rows = slice(ci * ch, (ci + 1) * ch)
        gl = DN_HEADS + h
        beta = bgc[ci][:, h:h + 1]
        decay = jnp.exp(jnp.where(incl, gam[ci][:, gl:gl + 1] - gam_t[ci][gl:gl + 1, :], -jnp.inf))
        q = q_s[h, rows, :]
        k = k_s[h, rows, :]
        kb = k * beta
        k16 = k.astype(BF16)
        eg = e_gam[ci][:, gl:gl + 1]
        a_list.append(_dot_nt(kb.astype(BF16), k16) * decay * cm_ref[_M_STRICT])
        qk16.append((_dot_nt(q.astype(BF16), k16) * decay).astype(BF16))
        vb16.append((v_s[h, rows, :] * beta).astype(BF16))
        kbe16.append((kb * eg).astype(BF16))
        qe16.append((q * eg).astype(BF16))
        kd16.append((k * e_rest[ci][:, gl:gl + 1]).astype(BF16))
    t16 = [t.astype(BF16) for t in _unit_lower_inverse(a_list, cm_ref)]
    u_list = [_dot(t, x) for t, x in zip(t16, vb16)]
    w16 = [_dot(t, x).astype(BF16) for t, x in zip(t16, kbe16)]

    gn = gn_ref[...]
    state = [s_ref[h] for h in range(DN_HEADS)]
    for ci in range(nc):
        rows = slice(ci * ch, (ci + 1) * ch)
        idx = [ci * DN_HEADS + h for h in range(DN_HEADS)]
        st16 = [st.astype(BF16) for st in state]
        vn16 = [(u_list[i] - _dot(w16[i], st)).astype(BF16) for i, st in zip(idx, st16)]
        o = [_dot(qe16[i], st) + _dot(qk16[i], vn) for i, st, vn in zip(idx, st16, vn16)]
        state = [st * e_last[ci][:, DN_HEADS + h:DN_HEADS + h + 1] + _dot_tn(kd16[i], vn)
                 for h, (i, st, vn) in enumerate(zip(idx, state, vn16))]
        for h in range(DN_HEADS):
            oh = o[h]
            on = oh * lax.rsqrt(jnp.mean(oh * oh, axis=-1, keepdims=True) + RMS_EPS) * gn
            zs = zs_ref[rows, h * hd:(h + 1) * hd].astype(F32)
            ya_ref[rows, h * hd:(h + 1) * hd] = (on * zs).astype(BF16)
    for h in range(DN_HEADS):
        s_ref[h] = state[h]


def _gated_delta(qkv, bg, bgt, zs, conv_w, dn_norm_g, batch, seq):
    t = batch * seq
    lt = GDN_TILE
    nt = seq // lt
    cm = _gdn_constants(GDN_CHUNK)
    w3 = qkv.shape[1]
    row = lambda w: pl.BlockSpec((lt, w), lambda b, c: (b * nt + c, 0))
    return pl.pallas_call(
        _gdn_kernel,
        out_shape=jax.ShapeDtypeStruct((t, DN_WIDTH), BF16),
        grid=(batch, nt),
        in_specs=[
            row(w3), row(LANES),
            pl.BlockSpec((2 * DN_HEADS, lt), lambda b, c: (0, b * nt + c)),
            row(DN_WIDTH),
            _const_spec(conv_w.shape), _const_spec((1, DN_HEAD_DIM)), _const_spec(cm.shape),
        ],
        out_specs=row(DN_WIDTH),
        scratch_shapes=[
            pltpu.VMEM((lt + SUBLANES, w3), F32),
            pltpu.VMEM((DN_HEADS, DN_HEAD_DIM, DN_HEAD_DIM), F32),
            pltpu.VMEM((DN_HEADS, lt, DN_HEAD_DIM), F32),
            pltpu.VMEM((DN_HEADS, lt, DN_HEAD_DIM), F32),
            pltpu.VMEM((DN_HEADS, lt, DN_HEAD_DIM), F32),
        ],
        compiler_params=_params(("parallel", "arbitrary")),
        name="gated_delta",
    )(qkv, bg, bgt, zs, conv_w.astype(F32), dn_norm_g.reshape(1, DN_HEAD_DIM).astype(F32), cm)


def _merge_kernel(x_ref, ya_ref, u_ref, v_ref, ga_ref, gb_ref, sgw_ref, sgb_ref, wdn_ref, wsg_ref,
                  wout_ref, g2_ref, wrt_ref, br_ref,
                  x1_ref, h2_ref, route_ref, rw_ref, cnt_ref,
                  yb_s, carry_s):
    i = pl.program_id(0)
    tm = x_ref.shape[0]
    ne = wrt_ref.shape[0]

    @pl.when(i == 0)
    def _():
        carry_s[...] = jnp.zeros_like(carry_s)

    tri = (lax.broadcasted_iota(jnp.int32, (SG_CHUNK, SG_CHUNK), 0)
           >= lax.broadcasted_iota(jnp.int32, (SG_CHUNK, SG_CHUNK), 1))
    for g in range(SG_GROUPS):
        wg = jnp.where(tri, sgw_ref[g], 0.0).astype(BF16)
        cols = slice(g * SG_GROUP_DIM, (g + 1) * SG_GROUP_DIM)
        for n in range(tm // SG_CHUNK):
            rows = slice(n * SG_CHUNK, (n + 1) * SG_CHUNK)
            mix = _dot(wg, v_ref[rows, cols]) + sgb_ref[:, cols]
            yb_s[rows, cols] = (u_ref[rows, cols].astype(F32) * mix).astype(BF16)

    da = _dot(ya_ref[...], wdn_ref[...])
    db = _dot(yb_s[...], wsg_ref[...])
    merged = ga_ref[...].astype(F32) * da + gb_ref[...].astype(F32) * db
    x1 = x_ref[...] + _dot(merged.astype(BF16), wout_ref[...])
    x1_ref[...] = x1
    ms = jnp.mean(x1 * x1, axis=-1, keepdims=True)
    h2 = x1 * lax.rsqrt(ms + RMS_EPS) * g2_ref[...]
    h2_ref[...] = _pack_bf16_pairs(h2)

    logits = _dot_nt(wrt_ref[...], h2.astype(BF16)) + br_ref[...]
    eidx = lax.broadcasted_iota(jnp.int32, (ne, tm), 0).astype(F32)
    work = logits
    onehot = jnp.zeros((ne, tm), F32)
    sels, tops, idxs = [], [], []
    for _ in range(TOP_K):
        m = jnp.max(work, axis=0, keepdims=True)
        idx = jnp.min(jnp.where(work == m, eidx, float(ne)), axis=0, keepdims=True)
        sel = eidx == idx
        work = jnp.where(sel, -jnp.inf, work)
        onehot = onehot + sel.astype(F32)
        sels.append(sel)
        tops.append(m)
        idxs.append(idx)
    ps = [jnp.exp(m - tops[0]) for m in tops]
    den = ps[0] + ps[1] + ps[2] + ps[3]
    before = (lax.broadcasted_iota(jnp.int32, (tm, tm), 0)
              < lax.broadcasted_iota(jnp.int32, (tm, tm), 1)).astype(BF16)
    cum = _dot(onehot.astype(BF16), before) + carry_s[:, 0:1]
    ranks = [jnp.sum(jnp.where(sel, cum, 0.0), axis=0, keepdims=True) for sel in sels]
    route_ref[...] = jnp.concatenate(idxs + ranks, axis=0).astype(jnp.int32)
    rw_ref[...] = jnp.concatenate([p / den for p in ps] + [jnp.zeros((TOP_K, tm), F32)], axis=0)
    carry_s[...] = carry_s[...] + jnp.sum(onehot, axis=1, keepdims=True)
    cnt_ref[...] = carry_s[...]


def _merge_route(x2, ya, u, v, ga, gb, sg_w, sg_b, w_o_dn, w_o_sg, w_out, norm2_g, w_router, b_router):
    t, d = x2.shape
    tm = ROW_TILE
    ne = w_router.shape[1]
    sgb_full = jnp.repeat(sg_b.T.astype(F32), SG_GROUP_DIM, axis=1)
    row = lambda w: pl.BlockSpec((tm, w), lambda i: (i, 0))
    col = lambda h: pl.BlockSpec((h, tm), lambda i: (0, i))
    out_shape = (
        jax.ShapeDtypeStruct((t, d), F32),
        jax.ShapeDtypeStruct((t, d // 2), jnp.int32),
        jax.ShapeDtypeStruct((2 * TOP_K, t), jnp.int32),
        jax.ShapeDtypeStruct((2 * TOP_K, t), F32),
        jax.ShapeDtypeStruct((ne, LANES), F32),
    )
    return pl.pallas_call(
        _merge_kernel,
        out_shape=out_shape,
        grid=(t // tm,),
        in_specs=[
            row(d), row(DN_WIDTH), row(SG_WIDTH), row(SG_WIDTH), row(d), row(d),
            _const_spec(sg_w.shape), _const_spec(sgb_full.shape),
            _const_spec(w_o_dn.shape), _const_spec(w_o_sg.shape), _const_spec(w_out.shape),
            _const_spec((1, d)), _const_spec((ne, d)), _const_spec((ne, 1)),
        ],
        out_specs=(row(d), row(d // 2), col(2 * TOP_K), col(2 * TOP_K), _const_spec((ne, LANES))),
        scratch_shapes=[pltpu.VMEM((tm, SG_WIDTH), BF16), pltpu.VMEM((ne, LANES), F32)],
        compiler_params=_params(("arbitrary",)),
        name="merge_route",
    )(x2, ya, u, v, ga, gb, sg_w.astype(F32), sgb_full, w_o_dn.astype(BF16), w_o_sg.astype(BF16),
      w_out.astype(BF16), norm2_g.reshape(1, d).astype(F32), w_router.T.astype(BF16),
      b_router.reshape(ne, 1).astype(F32))


def _sc_mesh():
    return plsc.VectorSubcoreMesh(core_axis_name="c", subcore_axis_name="s")


def _dispatch(h2, dest3, n_rows):
    n_tok, d = h2.shape
    chunk = dest3.shape[2]
    per_w = n_tok // SC_WORKERS
    n_chunks = per_w // chunk
    nbuf = 2

    @functools.partial(
        pl.kernel, mesh=_sc_mesh(),
        out_type=jax.ShapeDtypeStruct((n_rows, d), h2.dtype),
        scratch_types=[
            pltpu.VMEM((n_chunks, TOP_K, chunk), jnp.int32),
            pltpu.VMEM((nbuf, chunk, d), h2.dtype),
            pltpu.SemaphoreType.DMA((nbuf,)),
            pltpu.SemaphoreType.DMA((nbuf,)),
        ],
        name="moe_dispatch",
    )
    def k(h2_hbm, dest_hbm, xs_hbm, idx_v, rows_v, sem_in, sem_out):
        wid = lax.axis_index("s") * SC_CORES + lax.axis_index("c")
        base = wid * per_w
        pltpu.sync_copy(dest_hbm.at[pl.ds(wid * n_chunks, n_chunks)], idx_v)
        loads = [None] * n_chunks
        stores = [None] * n_chunks
        for j in range(n_chunks + 1):
            if j < n_chunks:
                b = j % nbuf
                if j >= nbuf:
                    for cp in stores[j - nbuf]:
                        cp.wait()
                t0 = pl.multiple_of(base + j * chunk, chunk)
                loads[j] = pltpu.async_copy(h2_hbm.at[pl.ds(t0, chunk)], rows_v.at[b], sem_in.at[b])
            if j >= 1:
                pb = (j - 1) % nbuf
                loads[j - 1].wait()
                stores[j - 1] = [
                    pltpu.async_copy(rows_v.at[pb], xs_hbm.at[idx_v.at[j - 1, kk]], sem_out.at[pb])
                    for kk in range(TOP_K)]
        for j in range(max(n_chunks - nbuf, 0), n_chunks):
            for cp in stores[j]:
                cp.wait()

    return k(h2, dest3)


def _collect(y, dest3):
    n_tok = dest3.shape[0] * dest3.shape[2]
    chunk = dest3.shape[2]
    d = y.shape[1]
    per_w = n_tok // SC_WORKERS
    n_chunks = per_w // chunk
    n_items = n_chunks * TOP_K
    nbuf = 3

    @functools.partial(
        pl.kernel, mesh=_sc_mesh(),
        out_type=jax.ShapeDtypeStruct((TOP_K, n_tok, d), y.dtype),
        scratch_types=[
            pltpu.VMEM((n_chunks, TOP_K, chunk), jnp.int32),
            pltpu.VMEM((nbuf, chunk, d), y.dtype),
            pltpu.SemaphoreType.DMA((nbuf,)),
            pltpu.SemaphoreType.DMA((nbuf,)),
        ],
        name="moe_collect",
    )
    def k(y_hbm, dest_hbm, yg_hbm, idx_v, rows_v, sem_in, sem_out):
        wid = lax.axis_index("s") * SC_CORES + lax.axis_index("c")
        base = wid * per_w
        pltpu.sync_copy(dest_hbm.at[pl.ds(wid * n_chunks, n_chunks)], idx_v)
        gathers = [None] * n_items
        stores = [None] * n_items
        for m in range(n_items + 1):
            if m < n_items:
                b = m % nbuf
                if m >= nbuf:
                    stores[m - nbuf].wait()
                j, kk = divmod(m, TOP_K)
                gathers[m] = pltpu.async_copy(y_hbm.at[idx_v.at[j, kk]], rows_v.at[b], sem_in.at[b])
            if m >= 1:
                pb = (m - 1) % nbuf
                j, kk = divmod(m - 1, TOP_K)
                t0 = pl.multiple_of(base + j * chunk, chunk)
                gathers[m - 1].wait()
                stores[m - 1] = pltpu.async_copy(rows_v.at[pb], yg_hbm.at[kk, pl.ds(t0, chunk)], sem_out.at[pb])
        for m in range(max(n_items - nbuf, 0), n_items):
            stores[m].wait()

    return k(y, dest3)


_BLK_VALID, _BLK_FIRST, _BLK_LAST, _BLK_HALF = 1, 2, 4, 8


def _expert_kernel(be_ref, nxt_ref, flag_ref, xs_ref, wgu_hbm, bgu_ref, wd_hbm, bd_ref, y_ref,
                   wgu_f, wd_f, wgu_s, wd_s, sem):
    i = pl.program_id(0)
    tm = xs_ref.shape[0]
    dff = wd_s.shape[0]
    flags = flag_ref[i]
    valid = (flags & _BLK_VALID) != 0
    has_next = nxt_ref[i] >= 0

    def weight_copies(e):
        return (pltpu.make_async_copy(wgu_hbm.at[e], wgu_f, sem.at[0]),
                pltpu.make_async_copy(wd_hbm.at[e], wd_f, sem.at[1]))

    def round_weights():
        wgu_s[...] = wgu_f[...].astype(BF16)
        wd_s[...] = wd_f[...].astype(BF16)

    @pl.when(i == 0)
    def _():
        for cp in weight_copies(be_ref[0]):
            cp.start()
        for cp in weight_copies(be_ref[0]):
            cp.wait()
        round_weights()

    @pl.when(jnp.logical_and((flags & _BLK_FIRST) != 0, has_next))
    def _():
        for cp in weight_copies(nxt_ref[i]):
            cp.start()

    def ffn(rows):
        x = _unpack_bf16_pairs(xs_ref[rows, :]).astype(BF16)
        gu = _dot(x, wgu_s[...]) + bgu_ref[0]
        x_glu = jnp.minimum(gu[:, :dff], SWIGLU_LIMIT)
        x_lin = jnp.clip(gu[:, dff:], -SWIGLU_LIMIT, SWIGLU_LIMIT)
        act = x_glu * _sigmoid(SWIGLU_ALPHA * x_glu) * (x_lin + 1.0)
        y_ref[rows, :] = _pack_bf16_pairs(_dot(act.astype(BF16), wd_s[...]) + bd_ref[0])

    half = (flags & _BLK_HALF) != 0

    @pl.when(jnp.logical_and(valid, jnp.logical_not(half)))
    def _():
        ffn(slice(0, tm))

    @pl.when(jnp.logical_and(valid, half))
    def _():
        ffn(slice(0, tm // 2))
        y_ref[tm // 2:, :] = jnp.zeros((tm - tm // 2, y_ref.shape[1]), y_ref.dtype)

    @pl.when(jnp.logical_not(valid))
    def _():
        y_ref[...] = jnp.zeros_like(y_ref)

    @pl.when(jnp.logical_and((flags & _BLK_LAST) != 0, has_next))
    def _():
        for cp in weight_copies(nxt_ref[i]):
            cp.wait()
        round_weights()


def _expert_ffn(xs, block_expert, block_next, block_flags, w_gate_up, b_gate_up, w_down, b_down):
    n_rows = xs.shape[0]
    tm = MOE_TILE
    ne, d, dff2 = w_gate_up.shape
    dff = dff2 // 2
    n_blocks = n_rows // tm
    grid_spec = pltpu.PrefetchScalarGridSpec(
        num_scalar_prefetch=3,
        grid=(n_blocks,),
        in_specs=[
            pl.BlockSpec((tm, d // 2), lambda i, be, nx, fl: (i, 0)),
            pl.BlockSpec(memory_space=pl.ANY),
            pl.BlockSpec((1, 1, dff2), lambda i, be, nx, fl: (be[i], 0, 0)),
            pl.BlockSpec(memory_space=pl.ANY),
            pl.BlockSpec((1, 1, d), lambda i, be, nx, fl: (be[i], 0, 0)),
        ],
        out_specs=pl.BlockSpec((tm, d // 2), lambda i, be, nx, fl: (i, 0)),
        scratch_shapes=[
            pltpu.VMEM((d, dff2), F32), pltpu.VMEM((dff, d), F32),
            pltpu.VMEM((d, dff2), BF16), pltpu.VMEM((dff, d), BF16),
            pltpu.SemaphoreType.DMA((2,)),
        ],
    )
    return pl.pallas_call(
        _expert_kernel,
        out_shape=jax.ShapeDtypeStruct((n_rows, d // 2), jnp.int32),
        grid_spec=grid_spec,
        compiler_params=_params(("arbitrary",)),
        name="expert_ffn",
    )(block_expert, block_next, block_flags, xs, w_gate_up, b_gate_up.reshape(ne, 1, dff2).astype(F32),
      w_down, b_down.reshape(ne, 1, d).astype(F32))


def _final_kernel(x1_ref, yg_ref, rw_ref, gf_ref, o_ref):
    acc = x1_ref[...]
    rw = rw_ref[...]
    for kk in range(TOP_K):
        acc = acc + rw[:, kk:kk + 1] * _unpack_bf16_pairs(yg_ref[kk])
    ms = jnp.mean(acc * acc, axis=-1, keepdims=True)
    o_ref[...] = acc * lax.rsqrt(ms + RMS_EPS) * gf_ref[...]


def _combine_final(x1, yg, rw_tok, norm_f_g):
    t, d = x1.shape
    tm = ROW_TILE
    return pl.pallas_call(
        _final_kernel,
        out_shape=jax.ShapeDtypeStruct((t, d), F32),
        grid=(t // tm,),
        in_specs=[
            pl.BlockSpec((tm, d), lambda i: (i, 0)),
            pl.BlockSpec((TOP_K, tm, d // 2), lambda i: (0, i, 0)),
            pl.BlockSpec((tm, 2 * TOP_K), lambda i: (i, 0)),
            _const_spec((1, d)),
        ],
        out_specs=pl.BlockSpec((tm, d), lambda i: (i, 0)),
        compiler_params=_params(("parallel",)),
        name="combine_final",
    )(x1, yg, rw_tok, norm_f_g.reshape(1, d).astype(F32))


def _routing_tables(route, counts, n_tok):
    tm = MOE_TILE
    n_blocks = (n_tok * TOP_K) // tm + N_EXPERTS
    cnt = counts[:, 0].astype(jnp.int32)
    padded = (cnt + tm - 1) // tm * tm
    pad_end = jnp.cumsum(padded)
    pad_start = pad_end - padded
    eid = route[:TOP_K]
    rank = route[TOP_K:]
    start = jnp.sum(jnp.where(eid[..., None] == jnp.arange(N_EXPERTS, dtype=jnp.int32),
                              pad_start, 0), axis=-1)
    dest = (start + rank).astype(jnp.int32)
    dest3 = dest.reshape(TOP_K, n_tok // SC_CHUNK, SC_CHUNK).transpose(1, 0, 2)
    n_valid = (pad_end[-1] // tm).astype(jnp.int32)
    blk = jnp.arange(n_blocks, dtype=jnp.int32)
    valid = blk < n_valid
    first_row = jnp.minimum(blk, n_valid - 1) * tm
    be = jnp.sum((pad_end[None, :] <= first_row[:, None]).astype(jnp.int32), axis=1)
    be = jnp.minimum(be, N_EXPERTS - 1).astype(jnp.int32)
    ids = jnp.arange(N_EXPERTS, dtype=jnp.int32)
    later = (ids[None, :] > ids[:, None]) & (padded[None, :] > 0)
    next_e = jnp.min(jnp.where(later, ids[None, :], N_EXPERTS), axis=1)
    next_e = jnp.where(next_e == N_EXPERTS, -1, next_e).astype(jnp.int32)
    nxt = next_e[be]
    is_first = valid & (first_row == pad_start[be])
    is_last = valid & (first_row + tm == pad_end[be])
    rows_used = jnp.clip(cnt[be] - (first_row - pad_start[be]), 0, tm)
    is_half = valid & (rows_used <= tm // 2)
    flags = (valid * _BLK_VALID + is_first * _BLK_FIRST + is_last * _BLK_LAST
             + is_half * _BLK_HALF).astype(jnp.int32)
    return dest3, be, nxt, flags, n_blocks * tm


def _layer(x, norm1_g, w_in, conv_w, dn_a_log, dn_dt_bias, dn_norm_g, w_o_dn, sg_ln_g, sg_ln_b,
           sg_w, sg_b, w_o_sg, w_out, norm2_g, w_router, b_router, w_gate_up, b_gate_up, w_down, b_down):
    bsz, seq, d = x.shape
    t = bsz * seq
    x2 = x.reshape(t, d)
    qkv, zs, bg, bgt, u, v, ga, gb = _project(x2, norm1_g, w_in, dn_a_log, dn_dt_bias, sg_ln_g, sg_ln_b)
    ya = _gated_delta(qkv, bg, bgt, zs, conv_w, dn_norm_g, bsz, seq)
    x1, h2, route, rw, counts = _merge_route(x2, ya, u, v, ga, gb, sg_w, sg_b, w_o_dn, w_o_sg, w_out,
                                             norm2_g, w_router, b_router)
    dest3, block_expert, block_next, block_flags, n_rows = _routing_tables(route, counts, t)
    xs = _dispatch(h2, dest3, n_rows)
    y = _expert_ffn(xs, block_expert, block_next, block_flags, w_gate_up, b_gate_up, w_down, b_down)
    yg = _collect(y, dest3)
    return x1, yg, rw.T


def kernel(x, norm1_g, w_in, conv_w, dn_a_log, dn_dt_bias, dn_norm_g, w_o_dn, sg_ln_g, sg_ln_b, sg_w, sg_b, w_o_sg, w_out, norm2_g, w_router, b_router, w_gate_up, b_gate_up, w_down, b_down, norm_f_g):
    bsz, seq, d = x.shape
    depth = norm1_g.shape[0]
    ident = jnp.ones((d,), F32)
    for layer in range(depth):
        x1, yg, rw_tok = _layer(
            x, norm1_g[layer], w_in[layer], conv_w[layer], dn_a_log[layer], dn_dt_bias[layer],
            dn_norm_g[layer], w_o_dn[layer], sg_ln_g[layer], sg_ln_b[layer], sg_w[layer], sg_b[layer],
            w_o_sg[layer], w_out[layer], norm2_g[layer], w_router[layer], b_router[layer],
            w_gate_up[layer], b_gate_up[layer], w_down[layer], b_down[layer])
        last = layer == depth - 1
        if not last:
            raise NotImplementedError("stacked layers need an un-normalised combine")
        x = _combine_final(x1, yg, rw_tok, norm_f_g if last else ident).reshape(bsz, seq, d)
    return x
```

```python
import functools

import jax
import jax.numpy as jnp
from jax import lax
from jax.experimental import pallas as pl
from jax.experimental.pallas import tpu as pltpu
from jax.experimental.pallas import tpu_sc as plsc

F32 = jnp.float32
BF16 = jnp.bfloat16

DN_HEADS = 4
DN_HEAD_DIM = 128
DN_WIDTH = DN_HEADS * DN_HEAD_DIM
CONV_WIDTH = 4
SG_GROUPS = 4
SG_GROUP_DIM = 128
SG_WIDTH = SG_GROUPS * SG_GROUP_DIM
SG_CHUNK = 128
N_EXPERTS = 32
TOP_K = 4
SWIGLU_LIMIT = 7.0
SWIGLU_ALPHA = 1.702
RMS_EPS = 1e-6
LN_EPS = 1e-5
L2_EPS = 1e-6

OFF_Z = 3 * DN_WIDTH
OFF_BETA = OFF_Z + DN_WIDTH
OFF_U = OFF_BETA + 2 * DN_HEADS

LANES = 128
SUBLANES = 8
SC_CORES = 2
SC_SUBCORES = 16
SC_WORKERS = SC_CORES * SC_SUBCORES
VMEM_LIMIT = 56 * 1024 * 1024

ROW_TILE = 512
GDN_TILE = 512
GDN_CHUNK = 128
MOE_TILE = 512
SC_CHUNK = 64

_NT = (((1,), (1,)), ((), ()))
_TN = (((0,), (0,)), ((), ()))


def _dot(a, b):
    return jnp.dot(a, b, preferred_element_type=F32)


def _dot_nt(a, b):
    return lax.dot_general(a, b, _NT, preferred_element_type=F32)


def _dot_tn(a, b):
    return lax.dot_general(a, b, _TN, preferred_element_type=F32)


def _sigmoid(x):
    return 1.0 / (1.0 + jnp.exp(-x))


def _softplus(x):
    return jnp.maximum(x, 0.0) + jnp.log1p(jnp.exp(-jnp.abs(x)))


def _split3(x):
    hi = x.astype(BF16)
    r = x - hi.astype(F32)
    mid = r.astype(BF16)
    lo = (r - mid.astype(F32)).astype(BF16)
    return hi, mid, lo


def _pack_bf16_pairs(x):
    n = x.shape[1] // 2
    lo = lax.bitcast_convert_type(x[:, :n].astype(BF16).astype(F32), jnp.int32)
    hi = lax.bitcast_convert_type(x[:, n:].astype(BF16).astype(F32), jnp.int32)
    return lax.shift_right_logical(lo, 16) | (hi & jnp.int32(-65536))


def _unpack_bf16_pairs(p):
    lo = lax.bitcast_convert_type(lax.shift_left(p, 16), F32)
    hi = lax.bitcast_convert_type(p & jnp.int32(-65536), F32)
    return jnp.concatenate([lo, hi], axis=1)


def _const_spec(shape):
    nd = len(shape)
    return pl.BlockSpec(shape, lambda *_: (0,) * nd)


def _params(sem):
    return pltpu.CompilerParams(dimension_semantics=sem, vmem_limit_bytes=VMEM_LIMIT)


def _proj_kernel(x_ref, g1_ref, wqkv_ref, wz_ref, wba_ref, wbat_ref, wuv_ref, wga_ref, wgb_ref,
                 alog_r_ref, dtb_r_ref, alog_c_ref, dtb_c_ref, lng_ref, lnb_ref, cw_ref,
                 qkv_ref, zs_ref, bg_ref, bgt_ref, u_ref, v_ref, ga_ref, gb_ref,
                 xbuf, *, tiles_per_seq):
    i = pl.program_id(0)
    tm = x_ref.shape[0]
    hd = DN_HEAD_DIM
    halo = SUBLANES
    x = x_ref[...]
    ms = jnp.mean(x * x, axis=-1, keepdims=True)
    h = (x * lax.rsqrt(ms + RMS_EPS) * g1_ref[...]).astype(BF16)

    @pl.when(i % tiles_per_seq == 0)
    def _():
        xbuf[0:halo, :] = jnp.zeros((halo, xbuf.shape[1]), F32)

    @pl.when(i % tiles_per_seq != 0)
    def _():
        xbuf[0:halo, :] = xbuf[tm:tm + halo, :]

    xbuf[halo:tm + halo, :] = _dot(h, wqkv_ref[...])
    cw = cw_ref[...]
    conv = cw[CONV_WIDTH - 1:CONV_WIDTH, :] * xbuf[halo:tm + halo, :]
    for kk in range(CONV_WIDTH - 1):
        conv = conv + cw[kk:kk + 1, :] * xbuf[pl.ds(halo - (CONV_WIDTH - 1) + kk, tm), :]
    act = conv * _sigmoid(conv)
    for hh in range(DN_HEADS):
        qh = act[:, hh * hd:(hh + 1) * hd]
        kh = act[:, DN_WIDTH + hh * hd:DN_WIDTH + (hh + 1) * hd]
        qn = qh * (lax.rsqrt(jnp.sum(qh * qh, axis=-1, keepdims=True) + L2_EPS) * (hd ** -0.5))
        kn = kh * lax.rsqrt(jnp.sum(kh * kh, axis=-1, keepdims=True) + L2_EPS)
        qkv_ref[:, hh * hd:(hh + 1) * hd] = qn.astype(BF16)
        qkv_ref[:, DN_WIDTH + hh * hd:DN_WIDTH + (hh + 1) * hd] = kn.astype(BF16)
    qkv_ref[:, 2 * DN_WIDTH:] = act[:, 2 * DN_WIDTH:].astype(BF16)

    z = _dot(h, wz_ref[...])
    zs_ref[...] = (z * _sigmoid(z)).astype(BF16)

    ba = _dot(h, wba_ref[...])
    lane = lax.broadcasted_iota(jnp.int32, ba.shape, 1)
    g_col = -jnp.exp(alog_r_ref[...]) * _softplus(ba + dtb_r_ref[...])
    bg_ref[...] = jnp.where(lane < DN_HEADS, _sigmoid(ba), g_col)
    bat = _dot_nt(wbat_ref[...], h)
    row = lax.broadcasted_iota(jnp.int32, bat.shape, 0)
    g_row = -jnp.exp(alog_c_ref[...]) * _softplus(bat + dtb_c_ref[...])
    bgt_ref[...] = jnp.where(row < DN_HEADS, _sigmoid(bat), g_row)

    uv = _dot(h, wuv_ref[...])
    ge = 0.5 * uv * (1.0 + lax.erf(uv * (0.5 ** 0.5)))
    u_ref[...] = ge[:, :SG_WIDTH].astype(BF16)
    v = ge[:, SG_WIDTH:]
    mu = jnp.mean(v, axis=-1, keepdims=True)
    vc = v - mu
    var = jnp.mean(vc * vc, axis=-1, keepdims=True)
    v_ref[...] = (vc * lax.rsqrt(var + LN_EPS) * lng_ref[...] + lnb_ref[...]).astype(BF16)

    ga_ref[...] = _sigmoid(_dot(h, wga_ref[...])).astype(BF16)
    gb_ref[...] = _sigmoid(_dot(h, wgb_ref[...])).astype(BF16)


def _project(x2, seq, norm1_g, w_in, conv_w, dn_a_log, dn_dt_bias, sg_ln_g, sg_ln_b):
    t, d = x2.shape
    tm = ROW_TILE
    wb = w_in.astype(BF16)
    wqkv = wb[:, :OFF_Z]
    wz = wb[:, OFF_Z:OFF_BETA]
    wba8 = wb[:, OFF_BETA:OFF_U]
    wba = jnp.pad(wba8, ((0, 0), (0, LANES - 2 * DN_HEADS)))
    wbat = wba8.T
    wuv = wb[:, OFF_U:OFF_U + 2 * SG_WIDTH]
    wga = wb[:, OFF_U + 2 * SG_WIDTH:OFF_U + 2 * SG_WIDTH + d]
    wgb = wb[:, OFF_U + 2 * SG_WIDTH + d:]
    zeros4 = jnp.zeros((DN_HEADS,), F32)
    alog8 = jnp.concatenate([zeros4, dn_a_log.astype(F32)])
    dtb8 = jnp.concatenate([zeros4, dn_dt_bias.astype(F32)])
    alog_r = jnp.pad(alog8, (0, LANES - 2 * DN_HEADS)).reshape(1, LANES)
    dtb_r = jnp.pad(dtb8, (0, LANES - 2 * DN_HEADS)).reshape(1, LANES)
    alog_c = alog8.reshape(2 * DN_HEADS, 1)
    dtb_c = dtb8.reshape(2 * DN_HEADS, 1)

    row = lambda w: pl.BlockSpec((tm, w), lambda i: (i, 0))
    out_shape = (
        jax.ShapeDtypeStruct((t, OFF_Z), BF16),
        jax.ShapeDtypeStruct((t, DN_WIDTH), BF16),
        jax.ShapeDtypeStruct((t, LANES), F32),
        jax.ShapeDtypeStruct((2 * DN_HEADS, t), F32),
        jax.ShapeDtypeStruct((t, SG_WIDTH), BF16),
        jax.ShapeDtypeStruct((t, SG_WIDTH), BF16),
        jax.ShapeDtypeStruct((t, d), BF16),
        jax.ShapeDtypeStruct((t, d), BF16),
    )
    return pl.pallas_call(
        functools.partial(_proj_kernel, tiles_per_seq=seq // tm),
        out_shape=out_shape,
        grid=(t // tm,),
        in_specs=[
            row(d), _const_spec((1, d)),
            _const_spec(wqkv.shape), _const_spec(wz.shape), _const_spec(wba.shape),
            _const_spec(wbat.shape), _const_spec(wuv.shape), _const_spec(wga.shape),
            _const_spec(wgb.shape),
            _const_spec((1, LANES)), _const_spec((1, LANES)),
            _const_spec((2 * DN_HEADS, 1)), _const_spec((2 * DN_HEADS, 1)),
            _const_spec((1, SG_WIDTH)), _const_spec((1, SG_WIDTH)), _const_spec(conv_w.shape),
        ],
        out_specs=(
            row(OFF_Z), row(DN_WIDTH), row(LANES),
            pl.BlockSpec((2 * DN_HEADS, tm), lambda i: (0, i)),
            row(SG_WIDTH), row(SG_WIDTH), row(d), row(d),
        ),
        scratch_shapes=[pltpu.VMEM((tm + SUBLANES, OFF_Z), F32)],
        compiler_params=_params(("arbitrary",)),
        name="in_proj",
    )(x2, norm1_g.reshape(1, d), wqkv, wz, wba, wbat, wuv, wga, wgb,
      alog_r, dtb_r, alog_c, dtb_c, sg_ln_g.reshape(1, SG_WIDTH), sg_ln_b.reshape(1, SG_WIDTH),
      conv_w.astype(F32))


_M_DIAG8, _M_OFF16, _M_OFF32, _M_OFF64, _M_OFF128, _M_INCL, _M_STRICT, _M_EYE, _M_UPPER = range(9)


def _gdn_constants(c):
    r = jnp.arange(c)[:, None]
    s = jnp.arange(c)[None, :]
    same = lambda b: (r // b) == (s // b)
    mats = [same(8)]
    b = 8
    while b < c:
        mats.append(same(2 * b) & ~same(b) & (r > s))
        b *= 2
    mats += [r >= s, r > s, r == s, r <= s]
    return jnp.stack(mats).astype(F32)


def _unit_lower_inverse(a_list, cm_ref):
    eye = cm_ref[_M_EYE]
    a0 = [(a * cm_ref[_M_DIAG8]).astype(BF16) for a in a_list]
    a2 = [_dot(x, x).astype(BF16) for x in a0]
    t = [eye - x.astype(F32) for x in a0]
    t = [ti + _dot(ti.astype(BF16), x) for ti, x in zip(t, a2)]
    a4 = [_dot(x, x).astype(BF16) for x in a2]
    t = [ti + _dot(ti.astype(BF16), x) for ti, x in zip(t, a4)]
    n_levels = cm_ref.shape[0] - 5
    for lvl in range(n_levels):
        off = [(a * cm_ref[_M_OFF16 + lvl]).astype(BF16) for a in a_list]
        tb = [ti.astype(BF16) for ti in t]
        x = [_dot(tbi, o).astype(BF16) for tbi, o in zip(tb, off)]
        t = [ti - _dot(xi, tbi) for ti, xi, tbi in zip(t, x, tb)]
    return t


def _gdn_kernel(qkv_ref, bg_ref, bgt_ref, zs_ref, gn_ref, cm_ref, ya_ref, s_ref):
    c = pl.program_id(1)
    lt = qkv_ref.shape[0]
    ch = cm_ref.shape[1]
    hd = DN_HEAD_DIM

    @pl.when(c == 0)
    def _():
        s_ref[...] = jnp.zeros_like(s_ref)

    nc = lt // ch
    lower16 = cm_ref[_M_INCL].astype(BF16)
    upper16 = cm_ref[_M_UPPER].astype(BF16)
    incl = cm_ref[_M_INCL] > 0.0

    bgc, gam, e_gam, e_rest, e_last, gam_t = [], [], [], [], [], []
    for ci in range(nc):
        rows = slice(ci * ch, (ci + 1) * ch)
        b = bg_ref[rows, :]
        hi, mid, lo = _split3(b)
        g = _dot(lower16, hi) + _dot(lower16, mid) + _dot(lower16, lo)
        g_last = g[ch - 1:ch, :]
        bgc.append(b)
        gam.append(g)
        e_gam.append(jnp.exp(g))
        e_rest.append(jnp.exp(g_last - g))
        e_last.append(jnp.exp(g_last))
        hi, mid, lo = _split3(bgt_ref[:, rows])
        gam_t.append(_dot(hi, upper16) + _dot(mid, upper16) + _dot(lo, upper16))

    items = [(ci, h) for ci in range(nc) for h in range(DN_HEADS)]
    a_list, qk16, vb16, kbe16, qe16, kd16 = [], [], [], [], [], []
    for ci, h in items:
        rows = slice(ci * ch, (ci + 1) * ch)
        gl = DN_HEADS + h
        beta = bgc[ci][:, h:h + 1]
        decay = jnp.exp(jnp.where(incl, gam[ci][:, gl:gl + 1] - gam_t[ci][gl:gl + 1, :], -jnp.inf))
        q16 = qkv_ref[rows, h * hd:(h + 1) * hd]
        k16 = qkv_ref[rows, DN_WIDTH + h * hd:DN_WIDTH + (h + 1) * hd]
        v = qkv_ref[rows, 2 * DN_WIDTH + h * hd:2 * DN_WIDTH + (h + 1) * hd].astype(F32)
        q = q16.astype(F32)
        k = k16.astype(F32)
        kb = k * beta
        eg = e_gam[ci][:, gl:gl + 1]
        a_list.append(_dot_nt(kb.astype(BF16), k16) * decay * cm_ref[_M_STRICT])
        qk16.append((_dot_nt(q16, k16) * decay).astype(BF16))
        vb16.append((v * beta).astype(BF16))
        kbe16.append((kb * eg).astype(BF16))
        qe16.append((q * eg).astype(BF16))
        kd16.append((k * e_rest[ci][:, gl:gl + 1]).astype(BF16))
    t16 = [t.astype(BF16) for t in _unit_lower_inverse(a_list, cm_ref)]
    u_list = [_dot(t, x) for t, x in zip(t16, vb16)]
    w16 = [_dot(t, x).astype(BF16) for t, x in zip(t16, kbe16)]

    gn = gn_ref[...]
    state = [s_ref[h] for h in range(DN_HEADS)]
    for ci in range(nc):
        rows = slice(ci * ch, (ci + 1) * ch)
        idx = [ci * DN_HEADS + h for h in range(DN_HEADS)]
        st16 = [st.astype(BF16) for st in state]
        vn16 = [(u_list[i] - _dot(w16[i], st)).astype(BF16) for i, st in zip(idx, st16)]
        o = [_dot(qe16[i], st) + _dot(qk16[i], vn) for i, st, vn in zip(idx, st16, vn16)]
        state = [st * e_last[ci][:, DN_HEADS + h:DN_HEADS + h + 1] + _dot_tn(kd16[i], vn)
                 for h, (i, st, vn) in enumerate(zip(idx, state, vn16))]
        for h in range(DN_HEADS):
            oh = o[h]
            on = oh * lax.rsqrt(jnp.mean(oh * oh, axis=-1, keepdims=True) + RMS_EPS) * gn
            zs = zs_ref[rows, h * hd:(h + 1) * hd].astype(F32)
            ya_ref[rows, h * hd:(h + 1) * hd] = (on * zs).astype(BF16)
    for h in range(DN_HEADS):
        s_ref[h] = state[h]


def _gated_delta(qkv, bg, bgt, zs, dn_norm_g, batch, seq):
    t = batch * seq
    lt = GDN_TILE
    nt = seq // lt
    cm = _gdn_constants(GDN_CHUNK)
    w3 = qkv.shape[1]
    row = lambda w: pl.BlockSpec((lt, w), lambda b, c: (b * nt + c, 0))
    return pl.pallas_call(
        _gdn_kernel,
        out_shape=jax.ShapeDtypeStruct((t, DN_WIDTH), BF16),
        grid=(batch, nt),
        in_specs=[
            row(w3), row(LANES),
            pl.BlockSpec((2 * DN_HEADS, lt), lambda b, c: (0, b * nt + c)),
            row(DN_WIDTH),
            _const_spec((1, DN_HEAD_DIM)), _const_spec(cm.shape),
        ],
        out_specs=row(DN_WIDTH),
        scratch_shapes=[pltpu.VMEM((DN_HEADS, DN_HEAD_DIM, DN_HEAD_DIM), F32)],
        compiler_params=_params(("parallel", "arbitrary")),
        name="gated_delta",
    )(qkv, bg, bgt, zs, dn_norm_g.reshape(1, DN_HEAD_DIM).astype(F32), cm)


def _merge_kernel(x_ref, ya_ref, u_ref, v_ref, ga_ref, gb_ref, sgw_ref, sgb_ref, wdn_ref, wsg_ref,
                  wout_ref, g2_ref, wrt_ref, br_ref,
                  x1_ref, h2_ref, route_ref, rw_ref, cnt_ref,
                  yb_s, carry_s):
    i = pl.program_id(0)
    tm = x_ref.shape[0]
    ne = wrt_ref.shape[0]

    @pl.when(i == 0)
    def _():
        carry_s[...] = jnp.zeros_like(carry_s)

    tri = (lax.broadcasted_iota(jnp.int32, (SG_CHUNK, SG_CHUNK), 0)
           >= lax.broadcasted_iota(jnp.int32, (SG_CHUNK, SG_CHUNK), 1))
    for g in range(SG_GROUPS):
        wg = jnp.where(tri, sgw_ref[g], 0.0).astype(BF16)
        cols = slice(g * SG_GROUP_DIM, (g + 1) * SG_GROUP_DIM)
        for n in range(tm // SG_CHUNK):
            rows = slice(n * SG_CHUNK, (n + 1) * SG_CHUNK)
            mix = _dot(wg, v_ref[rows, cols]) + sgb_ref[:, cols]
            yb_s[rows, cols] = (u_ref[rows, cols].astype(F32) * mix).astype(BF16)

    da = _dot(ya_ref[...], wdn_ref[...])
    db = _dot(yb_s[...], wsg_ref[...])
    merged = ga_ref[...].astype(F32) * da + gb_ref[...].astype(F32) * db
    x1 = x_ref[...] + _dot(merged.astype(BF16), wout_ref[...])
    x1_ref[...] = x1
    ms = jnp.mean(x1 * x1, axis=-1, keepdims=True)
    h2 = x1 * lax.rsqrt(ms + RMS_EPS) * g2_ref[...]
    h2_ref[...] = _pack_bf16_pairs(h2)

    logits = _dot_nt(wrt_ref[...], h2.astype(BF16)) + br_ref[...]
    eidx = lax.broadcasted_iota(jnp.int32, (ne, tm), 0).astype(F32)
    work = logits
    onehot = jnp.zeros((ne, tm), F32)
    sels, tops, idxs = [], [], []
    for _ in range(TOP_K):
        m = jnp.max(work, axis=0, keepdims=True)
        idx = jnp.min(jnp.where(work == m, eidx, float(ne)), axis=0, keepdims=True)
        sel = eidx == idx
        work = jnp.where(sel, -jnp.inf, work)
        onehot = onehot + sel.astype(F32)
        sels.append(sel)
        tops.append(m)
        idxs.append(idx)
    ps = [jnp.exp(m - tops[0]) for m in tops]
    den = ps[0] + ps[1] + ps[2] + ps[3]
    before = (lax.broadcasted_iota(jnp.int32, (tm, tm), 0)
              < lax.broadcasted_iota(jnp.int32, (tm, tm), 1)).astype(BF16)
    cum = _dot(onehot.astype(BF16), before) + carry_s[:, 0:1]
    ranks = [jnp.sum(jnp.where(sel, cum, 0.0), axis=0, keepdims=True) for sel in sels]
    route_ref[...] = jnp.concatenate(idxs + ranks, axis=0).astype(jnp.int32)
    rw_ref[...] = jnp.concatenate([p / den for p in ps] + [jnp.zeros((TOP_K, tm), F32)], axis=0)
    carry_s[...] = carry_s[...] + jnp.sum(onehot, axis=1, keepdims=True)
    cnt_ref[...] = carry_s[...]


def _merge_route(x2, ya, u, v, ga, gb, sg_w, sg_b, w_o_dn, w_o_sg, w_out, norm2_g, w_router, b_router):
    t, d = x2.shape
    tm = ROW_TILE
    ne = w_router.shape[1]
    sgb_full = jnp.repeat(sg_b.T.astype(F32), SG_GROUP_DIM, axis=1)
    row = lambda w: pl.BlockSpec((tm, w), lambda i: (i, 0))
    col = lambda h: pl.BlockSpec((h, tm), lambda i: (0, i))
    out_shape = (
        jax.ShapeDtypeStruct((t, d), F32),
        jax.ShapeDtypeStruct((t, d // 2), jnp.int32),
        jax.ShapeDtypeStruct((2 * TOP_K, t), jnp.int32),
        jax.ShapeDtypeStruct((2 * TOP_K, t), F32),
        jax.ShapeDtypeStruct((ne, LANES), F32),
    )
    return pl.pallas_call(
        _merge_kernel,
        out_shape=out_shape,
        grid=(t // tm,),
        in_specs=[
            row(d), row(DN_WIDTH), row(SG_WIDTH), row(SG_WIDTH), row(d), row(d),
            _const_spec(sg_w.shape), _const_spec(sgb_full.shape),
            _const_spec(w_o_dn.shape), _const_spec(w_o_sg.shape), _const_spec(w_out.shape),
            _const_spec((1, d)), _const_spec((ne, d)), _const_spec((ne, 1)),
        ],
        out_specs=(row(d), row(d // 2), col(2 * TOP_K), col(2 * TOP_K), _const_spec((ne, LANES))),
        scratch_shapes=[pltpu.VMEM((tm, SG_WIDTH), BF16), pltpu.VMEM((ne, LANES), F32)],
        compiler_params=_params(("arbitrary",)),
        name="merge_route",
    )(x2, ya, u, v, ga, gb, sg_w.astype(F32), sgb_full, w_o_dn.astype(BF16), w_o_sg.astype(BF16),
      w_out.astype(BF16), norm2_g.reshape(1, d).astype(F32), w_router.T.astype(BF16),
      b_router.reshape(ne, 1).astype(F32))


def _sc_mesh():
    return plsc.VectorSubcoreMesh(core_axis_name="c", subcore_axis_name="s")


def _dispatch(h2, dest3, n_rows):
    n_tok, d = h2.shape
    chunk = dest3.shape[2]
    per_w = n_tok // SC_WORKERS
    n_chunks = per_w // chunk
    nbuf = 2

    @functools.partial(
        pl.kernel, mesh=_sc_mesh(),
        out_type=jax.ShapeDtypeStruct((n_rows, d), h2.dtype),
        scratch_types=[
            pltpu.VMEM((n_chunks, TOP_K, chunk), jnp.int32),
            pltpu.VMEM((nbuf, chunk, d), h2.dtype),
            pltpu.SemaphoreType.DMA((nbuf,)),
            pltpu.SemaphoreType.DMA((nbuf,)),
        ],
        name="moe_dispatch",
    )
    def k(h2_hbm, dest_hbm, xs_hbm, idx_v, rows_v, sem_in, sem_out):
        wid = lax.axis_index("s") * SC_CORES + lax.axis_index("c")
        base = wid * per_w
        pltpu.sync_copy(dest_hbm.at[pl.ds(wid * n_chunks, n_chunks)], idx_v)
        loads = [None] * n_chunks
        stores = [None] * n_chunks
        for j in range(n_chunks + 1):
            if j < n_chunks:
                b = j % nbuf
                if j >= nbuf:
                    for cp in stores[j - nbuf]:
                        cp.wait()
                t0 = pl.multiple_of(base + j * chunk, chunk)
                loads[j] = pltpu.async_copy(h2_hbm.at[pl.ds(t0, chunk)], rows_v.at[b], sem_in.at[b])
            if j >= 1:
                pb = (j - 1) % nbuf
                loads[j - 1].wait()
                stores[j - 1] = [
                    pltpu.async_copy(rows_v.at[pb], xs_hbm.at[idx_v.at[j - 1, kk]], sem_out.at[pb])
                    for kk in range(TOP_K)]
        for j in range(max(n_chunks - nbuf, 0), n_chunks):
            for cp in stores[j]:
                cp.wait()

    return k(h2, dest3)


def _collect(y, dest3):
    n_tok = dest3.shape[0] * dest3.shape[2]
    chunk = dest3.shape[2]
    d = y.shape[1]
    per_w = n_tok // SC_WORKERS
    n_chunks = per_w // chunk
    n_items = n_chunks * TOP_K
    nbuf = 3

    @functools.partial(
        pl.kernel, mesh=_sc_mesh(),
        out_type=jax.ShapeDtypeStruct((TOP_K, n_tok, d), y.dtype),
        scratch_types=[
            pltpu.VMEM((n_chunks, TOP_K, chunk), jnp.int32),
            pltpu.VMEM((nbuf, chunk, d), y.dtype),
            pltpu.SemaphoreType.DMA((nbuf,)),
            pltpu.SemaphoreType.DMA((nbuf,)),
        ],
        name="moe_collect",
    )
    def k(y_hbm, dest_hbm, yg_hbm, idx_v, rows_v, sem_in, sem_out):
        wid = lax.axis_index("s") * SC_CORES + lax.axis_index("c")
        base = wid * per_w
        pltpu.sync_copy(dest_hbm.at[pl.ds(wid * n_chunks, n_chunks)], idx_v)
        gathers = [None] * n_items
        stores = [None] * n_items
        for m in range(n_items + 1):
            if m < n_items:
                b = m % nbuf
                if m >= nbuf:
                    stores[m - nbuf].wait()
                j, kk = divmod(m, TOP_K)
                gathers[m] = pltpu.async_copy(y_hbm.at[idx_v.at[j, kk]], rows_v.at[b], sem_in.at[b])
            if m >= 1:
                pb = (m - 1) % nbuf
                j, kk = divmod(m - 1, TOP_K)
                t0 = pl.multiple_of(base + j * chunk, chunk)
                gathers[m - 1].wait()
                stores[m - 1] = pltpu.async_copy(rows_v.at[pb], yg_hbm.at[kk, pl.ds(t0, chunk)], sem_out.at[pb])
        for m in range(max(n_items - nbuf, 0), n_items):
            stores[m].wait()

    return k(y, dest3)


_BLK_VALID, _BLK_FIRST, _BLK_LAST, _BLK_HALF = 1, 2, 4, 8


def _expert_kernel(be_ref, nxt_ref, flag_ref, xs_ref, wgu_hbm, bgu_ref, wd_hbm, bd_ref, y_ref,
                   wgu_f, wd_f, wgu_s, wd_s, sem):
    i = pl.program_id(0)
    tm = xs_ref.shape[0]
    dff = wd_s.shape[0]
    flags = flag_ref[i]
    valid = (flags & _BLK_VALID) != 0
    has_next = nxt_ref[i] >= 0

    def weight_copies(e):
        return (pltpu.make_async_copy(wgu_hbm.at[e], wgu_f, sem.at[0]),
                pltpu.make_async_copy(wd_hbm.at[e], wd_f, sem.at[1]))

    def round_weights():
        wgu_s[...] = wgu_f[...].astype(BF16)
        wd_s[...] = wd_f[...].astype(BF16)

    @pl.when(i == 0)
    def _():
        for cp in weight_copies(be_ref[0]):
            cp.start()
        for cp in weight_copies(be_ref[0]):
            cp.wait()
        round_weights()

    @pl.when(jnp.logical_and((flags & _BLK_FIRST) != 0, has_next))
    def _():
        for cp in weight_copies(nxt_ref[i]):
            cp.start()

    def ffn(rows):
        x = _unpack_bf16_pairs(xs_ref[rows, :]).astype(BF16)
        gu = _dot(x, wgu_s[...]) + bgu_ref[0]
        x_glu = jnp.minimum(gu[:, :dff], SWIGLU_LIMIT)
        x_lin = jnp.clip(gu[:, dff:], -SWIGLU_LIMIT, SWIGLU_LIMIT)
        act = x_glu * _sigmoid(SWIGLU_ALPHA * x_glu) * (x_lin + 1.0)
        y_ref[rows, :] = _pack_bf16_pairs(_dot(act.astype(BF16), wd_s[...]) + bd_ref[0])

    half = (flags & _BLK_HALF) != 0

    @pl.when(jnp.logical_and(valid, jnp.logical_not(half)))
    def _():
        ffn(slice(0, tm))

    @pl.when(jnp.logical_and(valid, half))
    def _():
        ffn(slice(0, tm // 2))
        y_ref[tm // 2:, :] = jnp.zeros((tm - tm // 2, y_ref.shape[1]), y_ref.dtype)

    @pl.when(jnp.logical_not(valid))
    def _():
        y_ref[...] = jnp.zeros_like(y_ref)

    @pl.when(jnp.logical_and((flags & _BLK_LAST) != 0, has_next))
    def _():
        for cp in weight_copies(nxt_ref[i]):
            cp.wait()
        round_weights()


def _expert_ffn(xs, block_expert, block_next, block_flags, w_gate_up, b_gate_up, w_down, b_down):
    n_rows = xs.shape[0]
    tm = MOE_TILE
    ne, d, dff2 = w_gate_up.shape
    dff = dff2 // 2
    n_blocks = n_rows // tm
    grid_spec = pltpu.PrefetchScalarGridSpec(
        num_scalar_prefetch=3,
        grid=(n_blocks,),
        in_specs=[
            pl.BlockSpec((tm, d // 2), lambda i, be, nx, fl: (i, 0)),
            pl.BlockSpec(memory_space=pl.ANY),
            pl.BlockSpec((1, 1, dff2), lambda i, be, nx, fl: (be[i], 0, 0)),
            pl.BlockSpec(memory_space=pl.ANY),
            pl.BlockSpec((1, 1, d), lambda i, be, nx, fl: (be[i], 0, 0)),
        ],
        out_specs=pl.BlockSpec((tm, d // 2), lambda i, be, nx, fl: (i, 0)),
        scratch_shapes=[
            pltpu.VMEM((d, dff2), F32), pltpu.VMEM((dff, d), F32),
            pltpu.VMEM((d, dff2), BF16), pltpu.VMEM((dff, d), BF16),
            pltpu.SemaphoreType.DMA((2,)),
        ],
    )
    return pl.pallas_call(
        _expert_kernel,
        out_shape=jax.ShapeDtypeStruct((n_rows, d // 2), jnp.int32),
        grid_spec=grid_spec,
        compiler_params=_params(("arbitrary",)),
        name="expert_ffn",
    )(block_expert, block_next, block_flags, xs, w_gate_up, b_gate_up.reshape(ne, 1, dff2).astype(F32),
      w_down, b_down.reshape(ne, 1, d).astype(F32))


def _final_kernel(x1_ref, yg_ref, rw_ref, gf_ref, o_ref):
    acc = x1_ref[...]
    rw = rw_ref[...]
    for kk in range(TOP_K):
        acc = acc + rw[:, kk:kk + 1] * _unpack_bf16_pairs(yg_ref[kk])
    ms = jnp.mean(acc * acc, axis=-1, keepdims=True)
    o_ref[...] = acc * lax.rsqrt(ms + RMS_EPS) * gf_ref[...]


def _combine_final(x1, yg, rw_tok, norm_f_g):
    t, d = x1.shape
    tm = ROW_TILE
    return pl.pallas_call(
        _final_kernel,
        out_shape=jax.ShapeDtypeStruct((t, d), F32),
        grid=(t // tm,),
        in_specs=[
            pl.BlockSpec((tm, d), lambda i: (i, 0)),
            pl.BlockSpec((TOP_K, tm, d // 2), lambda i: (0, i, 0)),
            pl.BlockSpec((tm, 2 * TOP_K), lambda i: (i, 0)),
            _const_spec((1, d)),
        ],
        out_specs=pl.BlockSpec((tm, d), lambda i: (i, 0)),
        compiler_params=_params(("parallel",)),
        name="combine_final",
    )(x1, yg, rw_tok, norm_f_g.reshape(1, d).astype(F32))


def _routing_tables(route, counts, n_tok):
    tm = MOE_TILE
    n_blocks = (n_tok * TOP_K) // tm + N_EXPERTS
    cnt = counts[:, 0].astype(jnp.int32)
    padded = (cnt + tm - 1) // tm * tm
    pad_end = jnp.cumsum(padded)
    pad_start = pad_end - padded
    eid = route[:TOP_K]
    rank = route[TOP_K:]
    start = jnp.sum(jnp.where(eid[..., None] == jnp.arange(N_EXPERTS, dtype=jnp.int32),
                              pad_start, 0), axis=-1)
    dest = (start + rank).astype(jnp.int32)
    dest3 = dest.reshape(TOP_K, n_tok // SC_CHUNK, SC_CHUNK).transpose(1, 0, 2)
    n_valid = (pad_end[-1] // tm).astype(jnp.int32)
    blk = jnp.arange(n_blocks, dtype=jnp.int32)
    valid = blk < n_valid
    first_row = jnp.minimum(blk, n_valid - 1) * tm
    ids = jnp.arange(N_EXPERTS, dtype=jnp.int32)
    fr = first_row[:, None]
    owner = (pad_start[None, :] <= fr) & (fr < pad_end[None, :])
    pick = lambda tbl: jnp.sum(jnp.where(owner, tbl[None, :], 0), axis=1).astype(jnp.int32)
    be = pick(ids)
    later = (ids[None, :] > ids[:, None]) & (padded[None, :] > 0)
    next_e = jnp.min(jnp.where(later, ids[None, :], N_EXPERTS), axis=1)
    nxt = pick(jnp.where(next_e == N_EXPERTS, -1, next_e))
    start_b = pick(pad_start)
    is_first = valid & (first_row == start_b)
    is_last = valid & (first_row + tm == pick(pad_end))
    rows_used = jnp.clip(pick(cnt) - (first_row - start_b), 0, tm)
    is_half = valid & (rows_used <= tm // 2)
    flags = (valid * _BLK_VALID + is_first * _BLK_FIRST + is_last * _BLK_LAST
             + is_half * _BLK_HALF).astype(jnp.int32)
    return dest3, be, nxt, flags, n_blocks * tm


def _layer(x, norm1_g, w_in, conv_w, dn_a_log, dn_dt_bias, dn_norm_g, w_o_dn, sg_ln_g, sg_ln_b,
           sg_w, sg_b, w_o_sg, w_out, norm2_g, w_router, b_router, w_gate_up, b_gate_up, w_down, b_down):
    bsz, seq, d = x.shape
    t = bsz * seq
    x2 = x.reshape(t, d)
    qkv, zs, bg, bgt, u, v, ga, gb = _project(x2, seq, norm1_g, w_in, conv_w, dn_a_log, dn_dt_bias,
                                              sg_ln_g, sg_ln_b)
    ya = _gated_delta(qkv, bg, bgt, zs, dn_norm_g, bsz, seq)
    x1, h2, route, rw, counts = _merge_route(x2, ya, u, v, ga, gb, sg_w, sg_b, w_o_dn, w_o_sg, w_out,
                                             norm2_g, w_router, b_router)
    dest3, block_expert, block_next, block_flags, n_rows = _routing_tables(route, counts, t)
    xs = _dispatch(h2, dest3, n_rows)
    y = _expert_ffn(xs, block_expert, block_next, block_flags, w_gate_up, b_gate_up, w_down, b_down)
    yg = _collect(y, dest3)
    return x1, yg, rw.T


def kernel(x, norm1_g, w_in, conv_w, dn_a_log, dn_dt_bias, dn_norm_g, w_o_dn, sg_ln_g, sg_ln_b, sg_w, sg_b, w_o_sg, w_out, norm2_g, w_router, b_router, w_gate_up, b_gate_up, w_down, b_down, norm_f_g):
    bsz, seq, d = x.shape
    depth = norm1_g.shape[0]
    ident = jnp.ones((d,), F32)
    for layer in range(depth):
        x1, yg, rw_tok = _layer(
            x, norm1_g[layer], w_in[layer], conv_w[layer], dn_a_log[layer], dn_dt_bias[layer],
            dn_norm_g[layer], w_o_dn[layer], sg_ln_g[layer], sg_ln_b[layer], sg_w[layer], sg_b[layer],
            w_o_sg[layer], w_out[layer], norm2_g[layer], w_router[layer], b_router[layer],
            w_gate_up[layer], b_gate_up[layer], w_down[layer], b_down[layer])
        last = layer == depth - 1
        if not last:
            raise NotImplementedError("stacked layers need an un-normalised combine")
        x = _combine_final(x1, yg, rw_tok, norm_f_g if last else ident).reshape(bsz, seq, d)
    return x
```

```python
import functools

import jax
import jax.numpy as jnp
from jax import lax
from jax.experimental import pallas as pl
from jax.experimental.pallas import tpu as pltpu
from jax.experimental.pallas import tpu_sc as plsc

F32 = jnp.float32
BF16 = jnp.bfloat16

DN_HEADS = 4
DN_HEAD_DIM = 128
DN_WIDTH = DN_HEADS * DN_HEAD_DIM
CONV_WIDTH = 4
SG_GROUPS = 4
SG_GROUP_DIM = 128
SG_WIDTH = SG_GROUPS * SG_GROUP_DIM
SG_CHUNK = 128
N_EXPERTS = 32
TOP_K = 4
SWIGLU_LIMIT = 7.0
SWIGLU_ALPHA = 1.702
RMS_EPS = 1e-6
LN_EPS = 1e-5
L2_EPS = 1e-6

OFF_Z = 3 * DN_WIDTH
OFF_BETA = OFF_Z + DN_WIDTH
OFF_U = OFF_BETA + 2 * DN_HEADS

LANES = 128
SUBLANES = 8
SC_CORES = 2
SC_SUBCORES = 16
SC_WORKERS = SC_CORES * SC_SUBCORES
VMEM_LIMIT = 56 * 1024 * 1024

PAD_U = OFF_BETA + LANES
PAD_GA = PAD_U + 2 * SG_WIDTH

ROW_TILE = 512
GDN_TILE = 512
GDN_CHUNK = 128
MOE_TILE = 512
SC_CHUNK = 64
COMBINE_PARTS = 4

_NT = (((1,), (1,)), ((), ()))
_TN = (((0,), (0,)), ((), ()))


def _dot(a, b):
    return jnp.dot(a, b, preferred_element_type=F32)


def _dot_nt(a, b):
    return lax.dot_general(a, b, _NT, preferred_element_type=F32)


def _dot_tn(a, b):
    return lax.dot_general(a, b, _TN, preferred_element_type=F32)


def _sigmoid(x):
    return 1.0 / (1.0 + jnp.exp(-x))


def _softplus(x):
    return jnp.maximum(x, 0.0) + jnp.log1p(jnp.exp(-jnp.abs(x)))


def _split3(x):
    hi = x.astype(BF16)
    r = x - hi.astype(F32)
    mid = r.astype(BF16)
    lo = (r - mid.astype(F32)).astype(BF16)
    return hi, mid, lo


def _pack_bf16_pairs(x):
    n = x.shape[1] // 2
    lo = lax.bitcast_convert_type(x[:, :n].astype(BF16).astype(F32), jnp.int32)
    hi = lax.bitcast_convert_type(x[:, n:].astype(BF16).astype(F32), jnp.int32)
    return lax.shift_right_logical(lo, 16) | (hi & jnp.int32(-65536))


def _unpack_bf16_pairs(p):
    lo = lax.bitcast_convert_type(lax.shift_left(p, 16), F32)
    hi = lax.bitcast_convert_type(p & jnp.int32(-65536), F32)
    return jnp.concatenate([lo, hi], axis=1)


def _const_spec(shape):
    nd = len(shape)
    return pl.BlockSpec(shape, lambda *_: (0,) * nd)


def _params(sem):
    return pltpu.CompilerParams(dimension_semantics=sem, vmem_limit_bytes=VMEM_LIMIT)


def _proj_kernel(x_ref, g1_ref, w_ref, wbat_ref,
                 alog_r_ref, dtb_r_ref, alog_c_ref, dtb_c_ref, lng_ref, lnb_ref, cw_ref,
                 qkv_ref, zs_ref, bg_ref, bgt_ref, u_ref, v_ref, ga_ref, gb_ref,
                 xq, xk, xv, *, tiles_per_seq):
    i = pl.program_id(0)
    tm = x_ref.shape[0]
    hd = DN_HEAD_DIM
    halo = SUBLANES
    x = x_ref[...]
    ms = jnp.mean(x * x, axis=-1, keepdims=True)
    h = (x * lax.rsqrt(ms + RMS_EPS) * g1_ref[...]).astype(BF16)

    xbufs = (xq, xk, xv)

    @pl.when(i % tiles_per_seq == 0)
    def _():
        for xb in xbufs:
            xb[0:halo, :] = jnp.zeros((halo, xb.shape[1]), F32)

    @pl.when(i % tiles_per_seq != 0)
    def _():
        for xb in xbufs:
            xb[0:halo, :] = xb[tm:tm + halo, :]

    def conv_silu(xb, cols):
        cw = cw_ref[:, cols]
        conv = cw[CONV_WIDTH - 1:CONV_WIDTH, :] * xb[halo:tm + halo, :]
        for kk in range(CONV_WIDTH - 1):
            conv = conv + cw[kk:kk + 1, :] * xb[pl.ds(halo - (CONV_WIDTH - 1) + kk, tm), :]
        return conv * _sigmoid(conv)

    def qk_epilogue(xb, cols, scale):
        act = conv_silu(xb, cols)
        for hh in range(DN_HEADS):
            a = act[:, hh * hd:(hh + 1) * hd]
            n = a * (lax.rsqrt(jnp.sum(a * a, axis=-1, keepdims=True) + L2_EPS) * scale)
            qkv_ref[:, cols.start + hh * hd:cols.start + (hh + 1) * hd] = n.astype(BF16)

    def v_epilogue(xb, cols):
        qkv_ref[:, cols] = conv_silu(xb, cols).astype(BF16)

    def z_epilogue(z):
        zs_ref[...] = (z * _sigmoid(z)).astype(BF16)

    def ba_epilogue(ba):
        lane = lax.broadcasted_iota(jnp.int32, ba.shape, 1)
        g_col = -jnp.exp(alog_r_ref[...]) * _softplus(ba + dtb_r_ref[...])
        bg_ref[...] = jnp.where(lane < DN_HEADS, _sigmoid(ba), g_col)

    def bat_epilogue(bat):
        row = lax.broadcasted_iota(jnp.int32, bat.shape, 0)
        g_row = -jnp.exp(alog_c_ref[...]) * _softplus(bat + dtb_c_ref[...])
        bgt_ref[...] = jnp.where(row < DN_HEADS, _sigmoid(bat), g_row)

    def gelu(t):
        return 0.5 * t * (1.0 + lax.erf(t * (0.5 ** 0.5)))

    def u_epilogue(t):
        u_ref[...] = gelu(t).astype(BF16)

    def v_sg_epilogue(t):
        v = gelu(t)
        mu = jnp.mean(v, axis=-1, keepdims=True)
        vc = v - mu
        var = jnp.mean(vc * vc, axis=-1, keepdims=True)
        v_ref[...] = (vc * lax.rsqrt(var + LN_EPS) * lng_ref[...] + lnb_ref[...]).astype(BF16)

    def gate_epilogue(o_ref, cols):
        def f(t):
            o_ref[:, cols] = _sigmoid(t).astype(BF16)
        return f

    seg = DN_WIDTH
    q_cols, k_cols, v_cols = (slice(j * seg, (j + 1) * seg) for j in range(3))

    def conv_stage(xb, cols, epilogue):
        def mm():
            xb[halo:tm + halo, :] = _dot(h, w_ref[:, cols])
        return mm, lambda _: epilogue()
    stages = [
        conv_stage(xq, q_cols, lambda: qk_epilogue(xq, q_cols, hd ** -0.5)),
        conv_stage(xk, k_cols, lambda: qk_epilogue(xk, k_cols, 1.0)),
        conv_stage(xv, v_cols, lambda: v_epilogue(xv, v_cols)),
        (lambda: _dot(h, w_ref[:, OFF_Z:OFF_BETA]), z_epilogue),
        (lambda: _dot(h, w_ref[:, OFF_BETA:PAD_U]), ba_epilogue),
        (lambda: _dot_nt(wbat_ref[...], h), bat_epilogue),
        (lambda: _dot(h, w_ref[:, PAD_U:PAD_U + SG_WIDTH]), u_epilogue),
        (lambda: _dot(h, w_ref[:, PAD_U + SG_WIDTH:PAD_GA]), v_sg_epilogue),
    ]
    d_out = ga_ref.shape[1]
    for o_ref, base in ((ga_ref, PAD_GA), (gb_ref, PAD_GA + d_out)):
        for j in range(d_out // seg):
            cols = slice(j * seg, (j + 1) * seg)
            wcols = slice(base + j * seg, base + (j + 1) * seg)
            stages.append((functools.partial(lambda c: _dot(h, w_ref[:, c]), wcols),
                           gate_epilogue(o_ref, cols)))
    pending = None
    for mm, epilogue in stages:
        res = mm()
        if pending is not None:
            pending[0](pending[1])
        pending = (epilogue, res)
    pending[0](pending[1])


def _project(x2, seq, norm1_g, w_in, conv_w, dn_a_log, dn_dt_bias, sg_ln_g, sg_ln_b):
    t, d = x2.shape
    tm = ROW_TILE
    wpad = jnp.concatenate(
        [w_in[:, :OFF_U], jnp.zeros((d, PAD_U - OFF_U), w_in.dtype), w_in[:, OFF_U:]], axis=1).astype(BF16)
    wbat = w_in[:, OFF_BETA:OFF_U].T.astype(BF16)
    zeros4 = jnp.zeros((DN_HEADS,), F32)
    alog8 = jnp.concatenate([zeros4, dn_a_log.astype(F32)])
    dtb8 = jnp.concatenate([zeros4, dn_dt_bias.astype(F32)])
    alog_r = jnp.pad(alog8, (0, LANES - 2 * DN_HEADS)).reshape(1, LANES)
    dtb_r = jnp.pad(dtb8, (0, LANES - 2 * DN_HEADS)).reshape(1, LANES)
    alog_c = alog8.reshape(2 * DN_HEADS, 1)
    dtb_c = dtb8.reshape(2 * DN_HEADS, 1)

    row = lambda w: pl.BlockSpec((tm, w), lambda i: (i, 0))
    out_shape = (
        jax.ShapeDtypeStruct((t, OFF_Z), BF16),
        jax.ShapeDtypeStruct((t, DN_WIDTH), BF16),
        jax.ShapeDtypeStruct((t, LANES), F32),
        jax.ShapeDtypeStruct((2 * DN_HEADS, t), F32),
        jax.ShapeDtypeStruct((t, SG_WIDTH), BF16),
        jax.ShapeDtypeStruct((t, SG_WIDTH), BF16),
        jax.ShapeDtypeStruct((t, d), BF16),
        jax.ShapeDtypeStruct((t, d), BF16),
    )
    return pl.pallas_call(
        functools.partial(_proj_kernel, tiles_per_seq=seq // tm),
        out_shape=out_shape,
        grid=(t // tm,),
        in_specs=[
            row(d), _const_spec((1, d)),
            _const_spec(wpad.shape), _const_spec(wbat.shape),
            _const_spec((1, LANES)), _const_spec((1, LANES)),
            _const_spec((2 * DN_HEADS, 1)), _const_spec((2 * DN_HEADS, 1)),
            _const_spec((1, SG_WIDTH)), _const_spec((1, SG_WIDTH)), _const_spec(conv_w.shape),
        ],
        out_specs=(
            row(OFF_Z), row(DN_WIDTH), row(LANES),
            pl.BlockSpec((2 * DN_HEADS, tm), lambda i: (0, i)),
            row(SG_WIDTH), row(SG_WIDTH), row(d), row(d),
        ),
        scratch_shapes=[pltpu.VMEM((tm + SUBLANES, DN_WIDTH), F32)] * 3,
        compiler_params=_params(("arbitrary",)),
        name="in_proj",
    )(x2, norm1_g.reshape(1, d), wpad, wbat,
      alog_r, dtb_r, alog_c, dtb_c, sg_ln_g.reshape(1, SG_WIDTH), sg_ln_b.reshape(1, SG_WIDTH),
      conv_w.astype(F32))


_M_DIAG8, _M_OFF16, _M_OFF32, _M_OFF64, _M_OFF128, _M_INCL, _M_STRICT, _M_EYE, _M_UPPER = range(9)


def _gdn_constants(c):
    r = jnp.arange(c)[:, None]
    s = jnp.arange(c)[None, :]
    same = lambda b: (r // b) == (s // b)
    mats = [same(8)]
    b = 8
    while b < c:
        mats.append(same(2 * b) & ~same(b) & (r > s))
        b *= 2
    mats += [r >= s, r > s, r == s, r <= s]
    return jnp.stack(mats).astype(F32)


def _unit_lower_inverse(a_list, cm_ref):
    eye = cm_ref[_M_EYE]
    a0 = [(a * cm_ref[_M_DIAG8]).astype(BF16) for a in a_list]
    a2 = [_dot(x, x).astype(BF16) for x in a0]
    t = [eye - x.astype(F32) for x in a0]
    t = [ti + _dot(ti.astype(BF16), x) for ti, x in zip(t, a2)]
    a4 = [_dot(x, x).astype(BF16) for x in a2]
    t = [ti + _dot(ti.astype(BF16), x) for ti, x in zip(t, a4)]
    n_levels = cm_ref.shape[0] - 5
    for lvl in range(n_levels):
        off = [(a * cm_ref[_M_OFF16 + lvl]).astype(BF16) for a in a_list]
        tb = [ti.astype(BF16) for ti in t]
        x = [_dot(tbi, o).astype(BF16) for tbi, o in zip(tb, off)]
        t = [ti - _dot(xi, tbi) for ti, xi, tbi in zip(t, x, tb)]
    return t


def _gdn_kernel(qkv_ref, bg_ref, bgt_ref, zs_ref, gn_ref, cm_ref, ya_ref, s_ref):
    c = pl.program_id(1)
    lt = qkv_ref.shape[0]
    ch = cm_ref.shape[1]
    hd = DN_HEAD_DIM

    @pl.when(c == 0)
    def _():
        s_ref[...] = jnp.zeros_like(s_ref)

    nc = lt // ch
    lower16 = cm_ref[_M_INCL].astype(BF16)
    upper16 = cm_ref[_M_UPPER].astype(BF16)
    incl = cm_ref[_M_INCL] > 0.0

    bgc, gam, e_gam, e_rest, e_last, gam_t = [], [], [], [], [], []
    for ci in range(nc):
        rows = slice(ci * ch, (ci + 1) * ch)
        b = bg_ref[rows, :]
        hi, mid, lo = _split3(b)
        g = _dot(lower16, hi) + _dot(lower16, mid) + _dot(lower16, lo)
        g_last = g[ch - 1:ch, :]
        bgc.append(b)
        gam.append(g)
        e_gam.append(jnp.exp(g))
        e_rest.append(jnp.exp(g_last - g))
        e_last.append(jnp.exp(g_last))
        hi, mid, lo = _split3(bgt_ref[:, rows])
        gam_t.append(_dot(hi, upper16) + _dot(mid, upper16) + _dot(lo, upper16))

    items = [(ci, h) for ci in range(nc) for h in range(DN_HEADS)]
    a_list, qk16, vb16, kbe16, qe16, kd16 = [], [], [], [], [], []
    for ci, h in items:
        rows = slice(ci * ch, (ci + 1) * ch)
        gl = DN_HEADS + h
        beta = bgc[ci][:, h:h + 1]
        decay = jnp.exp(jnp.where(incl, gam[ci][:, gl:gl + 1] - gam_t[ci][gl:gl + 1, :], -jnp.inf))
        q16 = qkv_ref[rows, h * hd:(h + 1) * hd]
        k16 = qkv_ref[rows, DN_WIDTH + h * hd:DN_WIDTH + (h + 1) * hd]
        v = qkv_ref[rows, 2 * DN_WIDTH + h * hd:2 * DN_WIDTH + (h + 1) * hd].astype(F32)
        q = q16.astype(F32)
        k = k16.astype(F32)
        kb = k * beta
        eg = e_gam[ci][:, gl:gl + 1]
        a_list.append(_dot_nt(kb.astype(BF16), k16) * decay * cm_ref[_M_STRICT])
        qk16.append((_dot_nt(q16, k16) * decay).astype(BF16))
        vb16.append((v * beta).astype(BF16))
        kbe16.append((kb * eg).astype(BF16))
        qe16.append((q * eg).astype(BF16))
        kd16.append((k * e_rest[ci][:, gl:gl + 1]).astype(BF16))
    t16 = [t.astype(BF16) for t in _unit_lower_inverse(a_list, cm_ref)]
    u_list = [_dot(t, x) for t, x in zip(t16, vb16)]
    w16 = [_dot(t, x).astype(BF16) for t, x in zip(t16, kbe16)]

    gn = gn_ref[...]
    state = [s_ref[h] for h in range(DN_HEADS)]
    for ci in range(nc):
        rows = slice(ci * ch, (ci + 1) * ch)
        idx = [ci * DN_HEADS + h for h in range(DN_HEADS)]
        st16 = [st.astype(BF16) for st in state]
        vn16 = [(u_list[i] - _dot(w16[i], st)).astype(BF16) for i, st in zip(idx, st16)]
        o = [_dot(qe16[i], st) + _dot(qk16[i], vn) for i, st, vn in zip(idx, st16, vn16)]
        state = [st * e_last[ci][:, DN_HEADS + h:DN_HEADS + h + 1] + _dot_tn(kd16[i], vn)
                 for h, (i, st, vn) in enumerate(zip(idx, state, vn16))]
        for h in range(DN_HEADS):
            oh = o[h]
            on = oh * lax.rsqrt(jnp.mean(oh * oh, axis=-1, keepdims=True) + RMS_EPS) * gn
            zs = zs_ref[rows, h * hd:(h + 1) * hd].astype(F32)
            ya_ref[rows, h * hd:(h + 1) * hd] = (on * zs).astype(BF16)
    for h in range(DN_HEADS):
        s_ref[h] = state[h]


def _gated_delta(qkv, bg, bgt, zs, dn_norm_g, batch, seq):
    t = batch * seq
    lt = GDN_TILE
    nt = seq // lt
    cm = _gdn_constants(GDN_CHUNK)
    w3 = qkv.shape[1]
    row = lambda w: pl.BlockSpec((lt, w), lambda b, c: (b * nt + c, 0))
    return pl.pallas_call(
        _gdn_kernel,
        out_shape=jax.ShapeDtypeStruct((t, DN_WIDTH), BF16),
        grid=(batch, nt),
        in_specs=[
            row(w3), row(LANES),
            pl.BlockSpec((2 * DN_HEADS, lt), lambda b, c: (0, b * nt + c)),
            row(DN_WIDTH),
            _const_spec((1, DN_HEAD_DIM)), _const_spec(cm.shape),
        ],
        out_specs=row(DN_WIDTH),
        scratch_shapes=[pltpu.VMEM((DN_HEADS, DN_HEAD_DIM, DN_HEAD_DIM), F32)],
        compiler_params=_params(("parallel", "arbitrary")),
        name="gated_delta",
    )(qkv, bg, bgt, zs, dn_norm_g.reshape(1, DN_HEAD_DIM).astype(F32), cm)


def _merge_kernel(x_ref, ya_ref, u_ref, v_ref, ga_ref, gb_ref, sgw_ref, sgb_ref, wdn_ref, wsg_ref,
                  wout_ref, g2_ref, wrt_ref, br_ref,
                  x1_ref, h2_ref, route_ref, rw_ref, cnt_ref,
                  yb_s, carry_s):
    i = pl.program_id(0)
    tm = x_ref.shape[0]
    ne = wrt_ref.shape[0]

    @pl.when(i == 0)
    def _():
        carry_s[...] = jnp.zeros_like(carry_s)

    tri = (lax.broadcasted_iota(jnp.int32, (SG_CHUNK, SG_CHUNK), 0)
           >= lax.broadcasted_iota(jnp.int32, (SG_CHUNK, SG_CHUNK), 1))
    for g in range(SG_GROUPS):
        wg = jnp.where(tri, sgw_ref[g], 0.0).astype(BF16)
        cols = slice(g * SG_GROUP_DIM, (g + 1) * SG_GROUP_DIM)
        for n in range(tm // SG_CHUNK):
            rows = slice(n * SG_CHUNK, (n + 1) * SG_CHUNK)
            mix = _dot(wg, v_ref[rows, cols]) + sgb_ref[:, cols]
            yb_s[rows, cols] = (u_ref[rows, cols].astype(F32) * mix).astype(BF16)

    da = _dot(ya_ref[...], wdn_ref[...])
    db = _dot(yb_s[...], wsg_ref[...])
    merged = ga_ref[...].astype(F32) * da + gb_ref[...].astype(F32) * db
    x1 = x_ref[...] + _dot(merged.astype(BF16), wout_ref[...])
    x1_ref[...] = x1
    ms = jnp.mean(x1 * x1, axis=-1, keepdims=True)
    h2 = x1 * lax.rsqrt(ms + RMS_EPS) * g2_ref[...]
    h2_ref[...] = _pack_bf16_pairs(h2)

    logits = _dot_nt(wrt_ref[...], h2.astype(BF16)) + br_ref[...]
    eidx = lax.broadcasted_iota(jnp.int32, (ne, tm), 0).astype(F32)
    work = logits
    onehot = jnp.zeros((ne, tm), F32)
    sels, tops, idxs = [], [], []
    for _ in range(TOP_K):
        m = jnp.max(work, axis=0, keepdims=True)
        idx = jnp.min(jnp.where(work == m, eidx, float(ne)), axis=0, keepdims=True)
        sel = eidx == idx
        work = jnp.where(sel, -jnp.inf, work)
        onehot = onehot + sel.astype(F32)
        sels.append(sel)
        tops.append(m)
        idxs.append(idx)
    ps = [jnp.exp(m - tops[0]) for m in tops]
    den = ps[0] + ps[1] + ps[2] + ps[3]
    before = (lax.broadcasted_iota(jnp.int32, (tm, tm), 0)
              < lax.broadcasted_iota(jnp.int32, (tm, tm), 1)).astype(BF16)
    cum = _dot(onehot.astype(BF16), before) + carry_s[:, 0:1]
    ranks = [jnp.sum(jnp.where(sel, cum, 0.0), axis=0, keepdims=True) for sel in sels]
    route_ref[...] = jnp.concatenate(idxs + ranks, axis=0).astype(jnp.int32)
    rw_ref[...] = jnp.concatenate([p / den for p in ps] + [jnp.zeros((TOP_K, tm), F32)], axis=0)
    carry_s[...] = carry_s[...] + jnp.sum(onehot, axis=1, keepdims=True)
    cnt_ref[...] = carry_s[...]


def _merge_route(x2, ya, u, v, ga, gb, sg_w, sg_b, w_o_dn, w_o_sg, w_out, norm2_g, w_router, b_router):
    t, d = x2.shape
    tm = ROW_TILE
    ne = w_router.shape[1]
    sgb_full = jnp.repeat(sg_b.T.astype(F32), SG_GROUP_DIM, axis=1)
    row = lambda w: pl.BlockSpec((tm, w), lambda i: (i, 0))
    col = lambda h: pl.BlockSpec((h, tm), lambda i: (0, i))
    out_shape = (
        jax.ShapeDtypeStruct((t, d), F32),
        jax.ShapeDtypeStruct((t, d // 2), jnp.int32),
        jax.ShapeDtypeStruct((2 * TOP_K, t), jnp.int32),
        jax.ShapeDtypeStruct((2 * TOP_K, t), F32),
        jax.ShapeDtypeStruct((ne, LANES), F32),
    )
    return pl.pallas_call(
        _merge_kernel,
        out_shape=out_shape,
        grid=(t // tm,),
        in_specs=[
            row(d), row(DN_WIDTH), row(SG_WIDTH), row(SG_WIDTH), row(d), row(d),
            _const_spec(sg_w.shape), _const_spec(sgb_full.shape),
            _const_spec(w_o_dn.shape), _const_spec(w_o_sg.shape), _const_spec(w_out.shape),
            _const_spec((1, d)), _const_spec((ne, d)), _const_spec((ne, 1)),
        ],
        out_specs=(row(d), row(d // 2), col(2 * TOP_K), col(2 * TOP_K), _const_spec((ne, LANES))),
        scratch_shapes=[pltpu.VMEM((tm, SG_WIDTH), BF16), pltpu.VMEM((ne, LANES), F32)],
        compiler_params=_params(("arbitrary",)),
        name="merge_route",
    )(x2, ya, u, v, ga, gb, sg_w.astype(F32), sgb_full, w_o_dn.astype(BF16), w_o_sg.astype(BF16),
      w_out.astype(BF16), norm2_g.reshape(1, d).astype(F32), w_router.T.astype(BF16),
      b_router.reshape(ne, 1).astype(F32))


def _sc_mesh():
    return plsc.VectorSubcoreMesh(core_axis_name="c", subcore_axis_name="s")


def _dispatch(h2, dest3, n_rows):
    n_tok, d = h2.shape
    chunk = dest3.shape[2]
    per_w = n_tok // SC_WORKERS
    n_chunks = per_w // chunk
    nbuf = 2

    @functools.partial(
        pl.kernel, mesh=_sc_mesh(),
        out_type=jax.ShapeDtypeStruct((n_rows, d), h2.dtype),
        scratch_types=[
            pltpu.VMEM((n_chunks, TOP_K, chunk), jnp.int32),
            pltpu.VMEM((nbuf, chunk, d), h2.dtype),
            pltpu.SemaphoreType.DMA((nbuf,)),
            pltpu.SemaphoreType.DMA((nbuf,)),
        ],
        name="moe_dispatch",
    )
    def k(h2_hbm, dest_hbm, xs_hbm, idx_v, rows_v, sem_in, sem_out):
        wid = lax.axis_index("s") * SC_CORES + lax.axis_index("c")
        base = wid * per_w
        pltpu.sync_copy(dest_hbm.at[pl.ds(wid * n_chunks, n_chunks)], idx_v)
        loads = [None] * n_chunks
        stores = [None] * n_chunks
        for j in range(n_chunks + 1):
            if j < n_chunks:
                b = j % nbuf
                if j >= nbuf:
                    for cp in stores[j - nbuf]:
                        cp.wait()
                t0 = pl.multiple_of(base + j * chunk, chunk)
                loads[j] = pltpu.async_copy(h2_hbm.at[pl.ds(t0, chunk)], rows_v.at[b], sem_in.at[b])
            if j >= 1:
                pb = (j - 1) % nbuf
                loads[j - 1].wait()
                stores[j - 1] = [
                    pltpu.async_copy(rows_v.at[pb], xs_hbm.at[idx_v.at[j - 1, kk]], sem_out.at[pb])
                    for kk in range(TOP_K)]
        for j in range(max(n_chunks - nbuf, 0), n_chunks):
            for cp in stores[j]:
                cp.wait()

    return k(h2, dest3)


def _collect(y, dest3):
    n_tok = dest3.shape[0] * dest3.shape[2]
    chunk = dest3.shape[2]
    d = y.shape[1]
    per_w = n_tok // SC_WORKERS
    n_chunks = per_w // chunk
    n_items = n_chunks * TOP_K
    nbuf = 3

    @functools.partial(
        pl.kernel, mesh=_sc_mesh(),
        out_type=jax.ShapeDtypeStruct((TOP_K, n_tok, d), y.dtype),
        scratch_types=[
            pltpu.VMEM((n_chunks, TOP_K, chunk), jnp.int32),
            pltpu.VMEM((nbuf, chunk, d), y.dtype),
            pltpu.SemaphoreType.DMA((nbuf,)),
            pltpu.SemaphoreType.DMA((nbuf,)),
        ],
        name="moe_collect",
    )
    def k(y_hbm, dest_hbm, yg_hbm, idx_v, rows_v, sem_in, sem_out):
        wid = lax.axis_index("s") * SC_CORES + lax.axis_index("c")
        base = wid * per_w
        pltpu.sync_copy(dest_hbm.at[pl.ds(wid * n_chunks, n_chunks)], idx_v)
        gathers = [None] * n_items
        stores = [None] * n_items
        for m in range(n_items + 1):
            if m < n_items:
                b = m % nbuf
                if m >= nbuf:
                    stores[m - nbuf].wait()
                j, kk = divmod(m, TOP_K)
                gathers[m] = pltpu.async_copy(y_hbm.at[idx_v.at[j, kk]], rows_v.at[b], sem_in.at[b])
            if m >= 1:
                pb = (m - 1) % nbuf
                j, kk = divmod(m - 1, TOP_K)
                t0 = pl.multiple_of(base + j * chunk, chunk)
                gathers[m - 1].wait()
                stores[m - 1] = pltpu.async_copy(rows_v.at[pb], yg_hbm.at[kk, pl.ds(t0, chunk)], sem_out.at[pb])
        for m in range(max(n_items - nbuf, 0), n_items):
            stores[m].wait()

    return k(y, dest3)


_BLK_VALID, _BLK_FIRST, _BLK_LAST, _BLK_HALF = 1, 2, 4, 8


def _expert_kernel(be_ref, nxt_ref, flag_ref, xs_ref, wgu_hbm, bgu_ref, wd_hbm, bd_ref, y_ref,
                   wgu_f, wd_f, wgu_s, wd_s, sem):
    i = pl.program_id(0)
    tm = xs_ref.shape[0]
    dff = wd_s.shape[0]
    flags = flag_ref[i]
    valid = (flags & _BLK_VALID) != 0
    has_next = nxt_ref[i] >= 0

    def weight_copies(e):
        return (pltpu.make_async_copy(wgu_hbm.at[e], wgu_f, sem.at[0]),
                pltpu.make_async_copy(wd_hbm.at[e], wd_f, sem.at[1]))

    def round_weights():
        wgu_s[...] = wgu_f[...].astype(BF16)
        wd_s[...] = wd_f[...].astype(BF16)

    @pl.when(i == 0)
    def _():
        for cp in weight_copies(be_ref[0]):
            cp.start()
        for cp in weight_copies(be_ref[0]):
            cp.wait()
        round_weights()

    @pl.when(jnp.logical_and((flags & _BLK_FIRST) != 0, has_next))
    def _():
        for cp in weight_copies(nxt_ref[i]):
            cp.start()

    def ffn(rows):
        x = _unpack_bf16_pairs(xs_ref[rows, :]).astype(BF16)
        gu = _dot(x, wgu_s[...]) + bgu_ref[0]
        x_glu = jnp.minimum(gu[:, :dff], SWIGLU_LIMIT)
        x_lin = jnp.clip(gu[:, dff:], -SWIGLU_LIMIT, SWIGLU_LIMIT)
        act = x_glu * _sigmoid(SWIGLU_ALPHA * x_glu) * (x_lin + 1.0)
        y_ref[rows, :] = _pack_bf16_pairs(_dot(act.astype(BF16), wd_s[...]) + bd_ref[0])

    half = (flags & _BLK_HALF) != 0

    @pl.when(jnp.logical_and(valid, jnp.logical_not(half)))
    def _():
        ffn(slice(0, tm))

    @pl.when(jnp.logical_and(valid, half))
    def _():
        ffn(slice(0, tm // 2))
        y_ref[tm // 2:, :] = jnp.zeros((tm - tm // 2, y_ref.shape[1]), y_ref.dtype)

    @pl.when(jnp.logical_not(valid))
    def _():
        y_ref[...] = jnp.zeros_like(y_ref)

    @pl.when(jnp.logical_and((flags & _BLK_LAST) != 0, has_next))
    def _():
        for cp in weight_copies(nxt_ref[i]):
            cp.wait()
        round_weights()


def _expert_ffn(xs, block_expert, block_next, block_flags, w_gate_up, b_gate_up, w_down, b_down):
    n_rows = xs.shape[0]
    tm = MOE_TILE
    ne, d, dff2 = w_gate_up.shape
    dff = dff2 // 2
    n_blocks = n_rows // tm
    grid_spec = pltpu.PrefetchScalarGridSpec(
        num_scalar_prefetch=3,
        grid=(n_blocks,),
        in_specs=[
            pl.BlockSpec((tm, d // 2), lambda i, be, nx, fl: (i, 0)),
            pl.BlockSpec(memory_space=pl.ANY),
            pl.BlockSpec((1, 1, dff2), lambda i, be, nx, fl: (be[i], 0, 0)),
            pl.BlockSpec(memory_space=pl.ANY),
            pl.BlockSpec((1, 1, d), lambda i, be, nx, fl: (be[i], 0, 0)),
        ],
        out_specs=pl.BlockSpec((tm, d // 2), lambda i, be, nx, fl: (i, 0)),
        scratch_shapes=[
            pltpu.VMEM((d, dff2), F32), pltpu.VMEM((dff, d), F32),
            pltpu.VMEM((d, dff2), BF16), pltpu.VMEM((dff, d), BF16),
            pltpu.SemaphoreType.DMA((2,)),
        ],
    )
    return pl.pallas_call(
        _expert_kernel,
        out_shape=jax.ShapeDtypeStruct((n_rows, d // 2), jnp.int32),
        grid_spec=grid_spec,
        compiler_params=_params(("arbitrary",)),
        name="expert_ffn",
    )(block_expert, block_next, block_flags, xs, w_gate_up, b_gate_up.reshape(ne, 1, dff2).astype(F32),
      w_down, b_down.reshape(ne, 1, d).astype(F32))


def _final_kernel(x1_ref, yg_ref, rw_ref, gf_ref, *rest):
    o_ref = rest[-1]
    acc = x1_ref[...]
    rw = rw_ref[...]
    for kk in range(TOP_K):
        acc = acc + rw[:, kk:kk + 1] * _unpack_bf16_pairs(yg_ref[kk])
    ms = jnp.mean(acc * acc, axis=-1, keepdims=True)
    o_ref[...] = acc * lax.rsqrt(ms + RMS_EPS) * gf_ref[...]


def _combine_final(x1, yg_part, rw_tok, norm_f_g, part, out_prev):
    t, d = x1.shape
    tm = ROW_TILE
    steps = yg_part.shape[1] // tm
    off = part * steps
    in_specs = [
        pl.BlockSpec((tm, d), lambda i: (i + off, 0)),
        pl.BlockSpec((TOP_K, tm, d // 2), lambda i: (0, i, 0)),
        pl.BlockSpec((tm, 2 * TOP_K), lambda i: (i + off, 0)),
        _const_spec((1, d)),
    ]
    args = [x1, yg_part, rw_tok, norm_f_g.reshape(1, d).astype(F32)]
    aliases = {}
    if out_prev is not None:
        in_specs.append(pl.BlockSpec(memory_space=pl.ANY))
        args.append(out_prev)
        aliases = {len(args) - 1: 0}
    return pl.pallas_call(
        _final_kernel,
        out_shape=jax.ShapeDtypeStruct((t, d), F32),
        grid=(steps,),
        in_specs=in_specs,
        out_specs=pl.BlockSpec((tm, d), lambda i: (i + off, 0)),
        input_output_aliases=aliases,
        compiler_params=_params(("parallel",)),
        name="combine_final",
    )(*args)


def _routing_tables(route, counts, n_tok):
    tm = MOE_TILE
    n_blocks = (n_tok * TOP_K) // tm + N_EXPERTS
    cnt = counts[:, 0].astype(jnp.int32)
    padded = (cnt + tm - 1) // tm * tm
    pad_end = jnp.cumsum(padded)
    pad_start = pad_end - padded
    eid = route[:TOP_K]
    rank = route[TOP_K:]
    start = jnp.sum(jnp.where(eid[..., None] == jnp.arange(N_EXPERTS, dtype=jnp.int32),
                              pad_start, 0), axis=-1)
    dest = (start + rank).astype(jnp.int32)
    dest3 = dest.reshape(TOP_K, n_tok // SC_CHUNK, SC_CHUNK).transpose(1, 0, 2)
    n_valid = (pad_end[-1] // tm).astype(jnp.int32)
    blk = jnp.arange(n_blocks, dtype=jnp.int32)
    valid = blk < n_valid
    first_row = jnp.minimum(blk, n_valid - 1) * tm
    ids = jnp.arange(N_EXPERTS, dtype=jnp.int32)
    fr = first_row[:, None]
    owner = (pad_start[None, :] <= fr) & (fr < pad_end[None, :])
    pick = lambda tbl: jnp.sum(jnp.where(owner, tbl[None, :], 0), axis=1).astype(jnp.int32)
    be = pick(ids)
    later = (ids[None, :] > ids[:, None]) & (padded[None, :] > 0)
    next_e = jnp.min(jnp.where(later, ids[None, :], N_EXPERTS), axis=1)
    nxt = pick(jnp.where(next_e == N_EXPERTS, -1, next_e))
    start_b = pick(pad_start)
    is_first = valid & (first_row == start_b)
    is_last = valid & (first_row + tm == pick(pad_end))
    rows_used = jnp.clip(pick(cnt) - (first_row - start_b), 0, tm)
    is_half = valid & (rows_used <= tm // 2)
    flags = (valid * _BLK_VALID + is_first * _BLK_FIRST + is_last * _BLK_LAST
             + is_half * _BLK_HALF).astype(jnp.int32)
    return dest3, be, nxt, flags, n_blocks * tm


def _layer(x, norm1_g, w_in, conv_w, dn_a_log, dn_dt_bias, dn_norm_g, w_o_dn, sg_ln_g, sg_ln_b,
           sg_w, sg_b, w_o_sg, w_out, norm2_g, w_router, b_router, w_gate_up, b_gate_up, w_down, b_down,
           norm_f_g):
    bsz, seq, d = x.shape
    t = bsz * seq
    x2 = x.reshape(t, d)
    qkv, zs, bg, bgt, u, v, ga, gb = _project(x2, seq, norm1_g, w_in, conv_w, dn_a_log, dn_dt_bias,
                                              sg_ln_g, sg_ln_b)
    ya = _gated_delta(qkv, bg, bgt, zs, dn_norm_g, bsz, seq)
    x1, h2, route, rw, counts = _merge_route(x2, ya, u, v, ga, gb, sg_w, sg_b, w_o_dn, w_o_sg, w_out,
                                             norm2_g, w_router, b_router)
    dest3, block_expert, block_next, block_flags, n_rows = _routing_tables(route, counts, t)
    xs = _dispatch(h2, dest3, n_rows)
    y = _expert_ffn(xs, block_expert, block_next, block_flags, w_gate_up, b_gate_up, w_down, b_down)
    rw_tok = rw.T
    chunks_per_part = dest3.shape[0] // COMBINE_PARTS
    out = None
    for part in range(COMBINE_PARTS):
        yg = _collect(y, dest3[part * chunks_per_part:(part + 1) * chunks_per_part])
        out = _combine_final(x1, yg, rw_tok, norm_f_g, part, out)
    return out.reshape(bsz, seq, d)


def kernel(x, norm1_g, w_in, conv_w, dn_a_log, dn_dt_bias, dn_norm_g, w_o_dn, sg_ln_g, sg_ln_b, sg_w, sg_b, w_o_sg, w_out, norm2_g, w_router, b_router, w_gate_up, b_gate_up, w_down, b_down, norm_f_g):
    assert norm1_g.shape[0] == 1, "stacked layers are not supported"
    take = lambda a: a[0]
    return _layer(
        x, take(norm1_g), take(w_in), take(conv_w), take(dn_a_log), take(dn_dt_bias), take(dn_norm_g),
        take(w_o_dn), take(sg_ln_g), take(sg_ln_b), take(sg_w), take(sg_b), take(w_o_sg), take(w_out),
        take(norm2_g), take(w_router), take(b_router), take(w_gate_up), take(b_gate_up), take(w_down),
        take(b_down), norm_f_g)
```

```python
import functools

import jax
import jax.numpy as jnp
from jax import lax
from jax.experimental import pallas as pl
from jax.experimental.pallas import tpu as pltpu
from jax.experimental.pallas import tpu_sc as plsc

F32 = jnp.float32
BF16 = jnp.bfloat16

DN_HEADS = 4
DN_HEAD_DIM = 128
DN_WIDTH = DN_HEADS * DN_HEAD_DIM
CONV_WIDTH = 4
SG_GROUPS = 4
SG_GROUP_DIM = 128
SG_WIDTH = SG_GROUPS * SG_GROUP_DIM
SG_CHUNK = 128
N_EXPERTS = 32
TOP_K = 4
SWIGLU_LIMIT = 7.0
SWIGLU_ALPHA = 1.702
RMS_EPS = 1e-6
LN_EPS = 1e-5
L2_EPS = 1e-6

OFF_Z = 3 * DN_WIDTH
OFF_BETA = OFF_Z + DN_WIDTH
OFF_U = OFF_BETA + 2 * DN_HEADS

LANES = 128
SUBLANES = 8
SC_CORES = 2
SC_SUBCORES = 16
SC_WORKERS = SC_CORES * SC_SUBCORES
VMEM_LIMIT = 56 * 1024 * 1024

PAD_U = OFF_BETA + LANES
PAD_GA = PAD_U + 2 * SG_WIDTH

ROW_TILE = 512
GDN_TILE = 512
GDN_CHUNK = 128
MOE_TILE = 512
SC_CHUNK = 64
COMBINE_PARTS = 4

_NT = (((1,), (1,)), ((), ()))
_TN = (((0,), (0,)), ((), ()))


def _dot(a, b):
    return jnp.dot(a, b, preferred_element_type=F32)


def _dot_nt(a, b):
    return lax.dot_general(a, b, _NT, preferred_element_type=F32)


def _dot_tn(a, b):
    return lax.dot_general(a, b, _TN, preferred_element_type=F32)


def _sigmoid(x):
    return 1.0 / (1.0 + jnp.exp(-x))


def _softplus(x):
    return jnp.maximum(x, 0.0) + jnp.log1p(jnp.exp(-jnp.abs(x)))


def _split3(x):
    hi = x.astype(BF16)
    r = x - hi.astype(F32)
    mid = r.astype(BF16)
    lo = (r - mid.astype(F32)).astype(BF16)
    return hi, mid, lo


def _pack_bf16_pairs(x):
    n = x.shape[1] // 2
    lo = lax.bitcast_convert_type(x[:, :n].astype(BF16).astype(F32), jnp.int32)
    hi = lax.bitcast_convert_type(x[:, n:].astype(BF16).astype(F32), jnp.int32)
    return lax.shift_right_logical(lo, 16) | (hi & jnp.int32(-65536))


def _unpack_bf16_pairs(p):
    lo = lax.bitcast_convert_type(lax.shift_left(p, 16), F32)
    hi = lax.bitcast_convert_type(p & jnp.int32(-65536), F32)
    return jnp.concatenate([lo, hi], axis=1)


def _const_spec(shape):
    nd = len(shape)
    return pl.BlockSpec(shape, lambda *_: (0,) * nd)


def _params(sem):
    return pltpu.CompilerParams(dimension_semantics=sem, vmem_limit_bytes=VMEM_LIMIT)


def _proj_kernel(x_ref, g1_ref, w_ref, wbat_ref,
                 alog_r_ref, dtb_r_ref, alog_c_ref, dtb_c_ref, lng_ref, lnb_ref, cw_ref,
                 qkv_ref, zs_ref, bg_ref, bgt_ref, u_ref, v_ref, ga_ref, gb_ref,
                 xq, xk, xv, *, tiles_per_seq):
    i = pl.program_id(0)
    tm = x_ref.shape[0]
    hd = DN_HEAD_DIM
    halo = SUBLANES
    x = x_ref[...]
    ms = jnp.mean(x * x, axis=-1, keepdims=True)
    h = (x * lax.rsqrt(ms + RMS_EPS) * g1_ref[...]).astype(BF16)

    xbufs = (xq, xk, xv)

    @pl.when(i % tiles_per_seq == 0)
    def _():
        for xb in xbufs:
            xb[0:halo, :] = jnp.zeros((halo, xb.shape[1]), F32)

    @pl.when(i % tiles_per_seq != 0)
    def _():
        for xb in xbufs:
            xb[0:halo, :] = xb[tm:tm + halo, :]

    def conv_silu(xb, cols):
        cw = cw_ref[:, cols]
        conv = cw[CONV_WIDTH - 1:CONV_WIDTH, :] * xb[halo:tm + halo, :]
        for kk in range(CONV_WIDTH - 1):
            conv = conv + cw[kk:kk + 1, :] * xb[pl.ds(halo - (CONV_WIDTH - 1) + kk, tm), :]
        return conv * _sigmoid(conv)

    def qk_epilogue(xb, cols, scale):
        act = conv_silu(xb, cols)
        for hh in range(DN_HEADS):
            a = act[:, hh * hd:(hh + 1) * hd]
            n = a * (lax.rsqrt(jnp.sum(a * a, axis=-1, keepdims=True) + L2_EPS) * scale)
            qkv_ref[:, cols.start + hh * hd:cols.start + (hh + 1) * hd] = n.astype(BF16)

    def v_epilogue(xb, cols):
        qkv_ref[:, cols] = conv_silu(xb, cols).astype(BF16)

    def z_epilogue(z):
        zs_ref[...] = (z * _sigmoid(z)).astype(BF16)

    def ba_epilogue(ba):
        lane = lax.broadcasted_iota(jnp.int32, ba.shape, 1)
        g_col = -jnp.exp(alog_r_ref[...]) * _softplus(ba + dtb_r_ref[...])
        bg_ref[...] = jnp.where(lane < DN_HEADS, _sigmoid(ba), g_col)

    def bat_epilogue(bat):
        row = lax.broadcasted_iota(jnp.int32, bat.shape, 0)
        g_row = -jnp.exp(alog_c_ref[...]) * _softplus(bat + dtb_c_ref[...])
        bgt_ref[...] = jnp.where(row < DN_HEADS, _sigmoid(bat), g_row)

    def gelu(t):
        return 0.5 * t * (1.0 + lax.erf(t * (0.5 ** 0.5)))

    def u_epilogue(t):
        u_ref[...] = gelu(t).astype(BF16)

    def v_sg_epilogue(t):
        v = gelu(t)
        mu = jnp.mean(v, axis=-1, keepdims=True)
        vc = v - mu
        var = jnp.mean(vc * vc, axis=-1, keepdims=True)
        v_ref[...] = (vc * lax.rsqrt(var + LN_EPS) * lng_ref[...] + lnb_ref[...]).astype(BF16)

    def gate_epilogue(o_ref, cols):
        def f(t):
            o_ref[:, cols] = _sigmoid(t).astype(BF16)
        return f

    seg = DN_WIDTH
    q_cols, k_cols, v_cols = (slice(j * seg, (j + 1) * seg) for j in range(3))

    def conv_stage(xb, cols, epilogue):
        def mm():
            xb[halo:tm + halo, :] = _dot(h, w_ref[:, cols])
        return mm, lambda _: epilogue()
    stages = [
        conv_stage(xq, q_cols, lambda: qk_epilogue(xq, q_cols, hd ** -0.5)),
        conv_stage(xk, k_cols, lambda: qk_epilogue(xk, k_cols, 1.0)),
        conv_stage(xv, v_cols, lambda: v_epilogue(xv, v_cols)),
        (lambda: _dot(h, w_ref[:, OFF_Z:OFF_BETA]), z_epilogue),
        (lambda: _dot(h, w_ref[:, OFF_BETA:PAD_U]), ba_epilogue),
        (lambda: _dot_nt(wbat_ref[...], h), bat_epilogue),
        (lambda: _dot(h, w_ref[:, PAD_U:PAD_U + SG_WIDTH]), u_epilogue),
        (lambda: _dot(h, w_ref[:, PAD_U + SG_WIDTH:PAD_GA]), v_sg_epilogue),
    ]
    d_out = ga_ref.shape[1]
    for o_ref, base in ((ga_ref, PAD_GA), (gb_ref, PAD_GA + d_out)):
        for j in range(d_out // seg):
            cols = slice(j * seg, (j + 1) * seg)
            wcols = slice(base + j * seg, base + (j + 1) * seg)
            stages.append((functools.partial(lambda c: _dot(h, w_ref[:, c]), wcols),
                           gate_epilogue(o_ref, cols)))
    pending = None
    for mm, epilogue in stages:
        res = mm()
        if pending is not None:
            pending[0](pending[1])
        pending = (epilogue, res)
    pending[0](pending[1])


def _project(x2, seq, norm1_g, w_in, conv_w, dn_a_log, dn_dt_bias, sg_ln_g, sg_ln_b):
    t, d = x2.shape
    tm = ROW_TILE
    wpad = jnp.concatenate(
        [w_in[:, :OFF_U], jnp.zeros((d, PAD_U - OFF_U), w_in.dtype), w_in[:, OFF_U:]], axis=1).astype(BF16)
    wbat = w_in[:, OFF_BETA:OFF_U].T.astype(BF16)
    zeros4 = jnp.zeros((DN_HEADS,), F32)
    alog8 = jnp.concatenate([zeros4, dn_a_log.astype(F32)])
    dtb8 = jnp.concatenate([zeros4, dn_dt_bias.astype(F32)])
    alog_r = jnp.pad(alog8, (0, LANES - 2 * DN_HEADS)).reshape(1, LANES)
    dtb_r = jnp.pad(dtb8, (0, LANES - 2 * DN_HEADS)).reshape(1, LANES)
    alog_c = alog8.reshape(2 * DN_HEADS, 1)
    dtb_c = dtb8.reshape(2 * DN_HEADS, 1)

    row = lambda w: pl.BlockSpec((tm, w), lambda i: (i, 0))
    out_shape = (
        jax.ShapeDtypeStruct((t, OFF_Z), BF16),
        jax.ShapeDtypeStruct((t, DN_WIDTH), BF16),
        jax.ShapeDtypeStruct((t, LANES), F32),
        jax.ShapeDtypeStruct((2 * DN_HEADS, t), F32),
        jax.ShapeDtypeStruct((t, SG_WIDTH), BF16),
        jax.ShapeDtypeStruct((t, SG_WIDTH), BF16),
        jax.ShapeDtypeStruct((t, d), BF16),
        jax.ShapeDtypeStruct((t, d), BF16),
    )
    return pl.pallas_call(
        functools.partial(_proj_kernel, tiles_per_seq=seq // tm),
        out_shape=out_shape,
        grid=(t // tm,),
        in_specs=[
            row(d), _const_spec((1, d)),
            _const_spec(wpad.shape), _const_spec(wbat.shape),
            _const_spec((1, LANES)), _const_spec((1, LANES)),
            _const_spec((2 * DN_HEADS, 1)), _const_spec((2 * DN_HEADS, 1)),
            _const_spec((1, SG_WIDTH)), _const_spec((1, SG_WIDTH)), _const_spec(conv_w.shape),
        ],
        out_specs=(
            row(OFF_Z), row(DN_WIDTH), row(LANES),
            pl.BlockSpec((2 * DN_HEADS, tm), lambda i: (0, i)),
            row(SG_WIDTH), row(SG_WIDTH), row(d), row(d),
        ),
        scratch_shapes=[pltpu.VMEM((tm + SUBLANES, DN_WIDTH), F32)] * 3,
        compiler_params=_params(("arbitrary",)),
        name="in_proj",
    )(x2, norm1_g.reshape(1, d), wpad, wbat,
      alog_r, dtb_r, alog_c, dtb_c, sg_ln_g.reshape(1, SG_WIDTH), sg_ln_b.reshape(1, SG_WIDTH),
      conv_w.astype(F32))


_M_DIAG8, _M_OFF16, _M_OFF32, _M_OFF64, _M_OFF128, _M_INCL, _M_STRICT, _M_EYE, _M_UPPER = range(9)


def _gdn_constants(c):
    r = jnp.arange(c)[:, None]
    s = jnp.arange(c)[None, :]
    same = lambda b: (r // b) == (s // b)
    mats = [same(8)]
    b = 8
    while b < c:
        mats.append(same(2 * b) & ~same(b) & (r > s))
        b *= 2
    mats += [r >= s, r > s, r == s, r <= s]
    return jnp.stack(mats).astype(F32)


def _unit_lower_inverse(a_list, cm_ref):
    eye = cm_ref[_M_EYE]
    a0 = [(a * cm_ref[_M_DIAG8]).astype(BF16) for a in a_list]
    a2 = [_dot(x, x).astype(BF16) for x in a0]
    t = [eye - x.astype(F32) for x in a0]
    t = [ti + _dot(ti.astype(BF16), x) for ti, x in zip(t, a2)]
    a4 = [_dot(x, x).astype(BF16) for x in a2]
    t = [ti + _dot(ti.astype(BF16), x) for ti, x in zip(t, a4)]
    n_levels = cm_ref.shape[0] - 5
    for lvl in range(n_levels):
        off = [(a * cm_ref[_M_OFF16 + lvl]).astype(BF16) for a in a_list]
        tb = [ti.astype(BF16) for ti in t]
        x = [_dot(tbi, o).astype(BF16) for tbi, o in zip(tb, off)]
        t = [ti - _dot(xi, tbi) for ti, xi, tbi in zip(t, x, tb)]
    return t


def _gdn_kernel(qkv_ref, bg_ref, bgt_ref, zs_ref, gn_ref, cm_ref, ya_ref, s_ref):
    c = pl.program_id(1)
    lt = qkv_ref.shape[0]
    ch = cm_ref.shape[1]
    hd = DN_HEAD_DIM

    @pl.when(c == 0)
    def _():
        s_ref[...] = jnp.zeros_like(s_ref)

    nc = lt // ch
    lower16 = cm_ref[_M_INCL].astype(BF16)
    upper16 = cm_ref[_M_UPPER].astype(BF16)
    incl = cm_ref[_M_INCL] > 0.0

    bgc, gam, e_gam, e_rest, e_last, gam_t = [], [], [], [], [], []
    for ci in range(nc):
        rows = slice(ci * ch, (ci + 1) * ch)
        b = bg_ref[rows, :]
        hi, mid, lo = _split3(b)
        g = _dot(lower16, hi) + _dot(lower16, mid) + _dot(lower16, lo)
        g_last = g[ch - 1:ch, :]
        bgc.append(b)
        gam.append(g)
        e_gam.append(jnp.exp(g))
        e_rest.append(jnp.exp(g_last - g))
        e_last.append(jnp.exp(g_last))
        hi, mid, lo = _split3(bgt_ref[:, rows])
        gam_t.append(_dot(hi, upper16) + _dot(mid, upper16) + _dot(lo, upper16))

    items = [(ci, h) for ci in range(nc) for h in range(DN_HEADS)]
    a_list, qk16, vb16, kbe16, qe16, kd16 = [], [], [], [], [], []
    for ci, h in items:
        rows = slice(ci * ch, (ci + 1) * ch)
        gl = DN_HEADS + h
        beta = bgc[ci][:, h:h + 1]
        decay = jnp.exp(jnp.where(incl, gam[ci][:, gl:gl + 1] - gam_t[ci][gl:gl + 1, :], -jnp.inf))
        q16 = qkv_ref[rows, h * hd:(h + 1) * hd]
        k16 = qkv_ref[rows, DN_WIDTH + h * hd:DN_WIDTH + (h + 1) * hd]
        v = qkv_ref[rows, 2 * DN_WIDTH + h * hd:2 * DN_WIDTH + (h + 1) * hd].astype(F32)
        q = q16.astype(F32)
        k = k16.astype(F32)
        kb = k * beta
        eg = e_gam[ci][:, gl:gl + 1]
        a_list.append(_dot_nt(kb.astype(BF16), k16) * decay * cm_ref[_M_STRICT])
        qk16.append((_dot_nt(q16, k16) * decay).astype(BF16))
        vb16.append((v * beta).astype(BF16))
        kbe16.append((kb * eg).astype(BF16))
        qe16.append((q * eg).astype(BF16))
        kd16.append((k * e_rest[ci][:, gl:gl + 1]).astype(BF16))
    t16 = [t.astype(BF16) for t in _unit_lower_inverse(a_list, cm_ref)]
    u_list = [_dot(t, x) for t, x in zip(t16, vb16)]
    w16 = [_dot(t, x).astype(BF16) for t, x in zip(t16, kbe16)]

    gn = gn_ref[...]
    state = [s_ref[h] for h in range(DN_HEADS)]
    for ci in range(nc):
        rows = slice(ci * ch, (ci + 1) * ch)
        idx = [ci * DN_HEADS + h for h in range(DN_HEADS)]
        st16 = [st.astype(BF16) for st in state]
        vn16 = [(u_list[i] - _dot(w16[i], st)).astype(BF16) for i, st in zip(idx, st16)]
        o = [_dot(qe16[i], st) + _dot(qk16[i], vn) for i, st, vn in zip(idx, st16, vn16)]
        state = [st * e_last[ci][:, DN_HEADS + h:DN_HEADS + h + 1] + _dot_tn(kd16[i], vn)
                 for h, (i, st, vn) in enumerate(zip(idx, state, vn16))]
        for h in range(DN_HEADS):
            oh = o[h]
            on = oh * lax.rsqrt(jnp.mean(oh * oh, axis=-1, keepdims=True) + RMS_EPS) * gn
            zs = zs_ref[rows, h * hd:(h + 1) * hd].astype(F32)
            ya_ref[rows, h * hd:(h + 1) * hd] = (on * zs).astype(BF16)
    for h in range(DN_HEADS):
        s_ref[h] = state[h]


def _gated_delta(qkv, bg, bgt, zs, dn_norm_g, batch, seq):
    t = batch * seq
    lt = GDN_TILE
    nt = seq // lt
    cm = _gdn_constants(GDN_CHUNK)
    w3 = qkv.shape[1]
    row = lambda w: pl.BlockSpec((lt, w), lambda b, c: (b * nt + c, 0))
    return pl.pallas_call(
        _gdn_kernel,
        out_shape=jax.ShapeDtypeStruct((t, DN_WIDTH), BF16),
        grid=(batch, nt),
        in_specs=[
            row(w3), row(LANES),
            pl.BlockSpec((2 * DN_HEADS, lt), lambda b, c: (0, b * nt + c)),
            row(DN_WIDTH),
            _const_spec((1, DN_HEAD_DIM)), _const_spec(cm.shape),
        ],
        out_specs=row(DN_WIDTH),
        scratch_shapes=[pltpu.VMEM((DN_HEADS, DN_HEAD_DIM, DN_HEAD_DIM), F32)],
        compiler_params=_params(("parallel", "arbitrary")),
        name="gated_delta",
    )(qkv, bg, bgt, zs, dn_norm_g.reshape(1, DN_HEAD_DIM).astype(F32), cm)


def _merge_kernel(x_ref, ya_ref, u_ref, v_ref, ga_ref, gb_ref, sgw_ref, sgb_ref, wdn_ref, wsg_ref,
                  wout_ref, g2_ref, wrt_ref, br_ref,
                  x1_ref, h2_ref, route_ref, rw_ref, cnt_ref,
                  yb_s, carry_s, before_s):
    i = pl.program_id(0)
    tm = x_ref.shape[0]
    ne = wrt_ref.shape[0]

    @pl.when(i == 0)
    def _():
        carry_s[...] = jnp.zeros_like(carry_s)
        before_s[...] = (lax.broadcasted_iota(jnp.int32, (tm, tm), 0)
                         < lax.broadcasted_iota(jnp.int32, (tm, tm), 1)).astype(BF16)

    tri = (lax.broadcasted_iota(jnp.int32, (SG_CHUNK, SG_CHUNK), 0)
           >= lax.broadcasted_iota(jnp.int32, (SG_CHUNK, SG_CHUNK), 1))
    for g in range(SG_GROUPS):
        wg = jnp.where(tri, sgw_ref[g], 0.0).astype(BF16)
        cols = slice(g * SG_GROUP_DIM, (g + 1) * SG_GROUP_DIM)
        for n in range(tm // SG_CHUNK):
            rows = slice(n * SG_CHUNK, (n + 1) * SG_CHUNK)
            mix = _dot(wg, v_ref[rows, cols]) + sgb_ref[:, cols]
            yb_s[rows, cols] = (u_ref[rows, cols].astype(F32) * mix).astype(BF16)

    da = _dot(ya_ref[...], wdn_ref[...])
    db = _dot(yb_s[...], wsg_ref[...])
    merged = ga_ref[...].astype(F32) * da + gb_ref[...].astype(F32) * db
    x1 = x_ref[...] + _dot(merged.astype(BF16), wout_ref[...])
    x1_ref[...] = x1
    ms = jnp.mean(x1 * x1, axis=-1, keepdims=True)
    h2 = x1 * lax.rsqrt(ms + RMS_EPS) * g2_ref[...]
    h2_ref[...] = _pack_bf16_pairs(h2)

    logits = _dot_nt(wrt_ref[...], h2.astype(BF16)) + br_ref[...]
    eidx = lax.broadcasted_iota(jnp.int32, (ne, tm), 0).astype(F32)
    work = logits
    onehot = jnp.zeros((ne, tm), F32)
    sels, tops, idxs = [], [], []
    for _ in range(TOP_K):
        m = jnp.max(work, axis=0, keepdims=True)
        idx = jnp.min(jnp.where(work == m, eidx, float(ne)), axis=0, keepdims=True)
        sel = eidx == idx
        work = jnp.where(sel, -jnp.inf, work)
        onehot = onehot + sel.astype(F32)
        sels.append(sel)
        tops.append(m)
        idxs.append(idx)
    ps = [jnp.exp(m - tops[0]) for m in tops]
    den = ps[0] + ps[1] + ps[2] + ps[3]
    cum = _dot(onehot.astype(BF16), before_s[...]) + carry_s[:, 0:1]
    ranks = [jnp.sum(jnp.where(sel, cum, 0.0), axis=0, keepdims=True) for sel in sels]
    route_ref[...] = jnp.concatenate(idxs + ranks, axis=0).astype(jnp.int32)
    rw_ref[...] = jnp.concatenate([p / den for p in ps] + [jnp.zeros((TOP_K, tm), F32)], axis=0)
    carry_s[...] = carry_s[...] + jnp.sum(onehot, axis=1, keepdims=True)
    cnt_ref[...] = carry_s[...]


def _merge_route(x2, ya, u, v, ga, gb, sg_w, sg_b, w_o_dn, w_o_sg, w_out, norm2_g, w_router, b_router):
    t, d = x2.shape
    tm = ROW_TILE
    ne = w_router.shape[1]
    sgb_full = jnp.repeat(sg_b.T.astype(F32), SG_GROUP_DIM, axis=1)
    row = lambda w: pl.BlockSpec((tm, w), lambda i: (i, 0))
    col = lambda h: pl.BlockSpec((h, tm), lambda i: (0, i))
    out_shape = (
        jax.ShapeDtypeStruct((t, d), F32),
        jax.ShapeDtypeStruct((t, d // 2), jnp.int32),
        jax.ShapeDtypeStruct((2 * TOP_K, t), jnp.int32),
        jax.ShapeDtypeStruct((2 * TOP_K, t), F32),
        jax.ShapeDtypeStruct((ne, LANES), F32),
    )
    return pl.pallas_call(
        _merge_kernel,
        out_shape=out_shape,
        grid=(t // tm,),
        in_specs=[
            row(d), row(DN_WIDTH), row(SG_WIDTH), row(SG_WIDTH), row(d), row(d),
            _const_spec(sg_w.shape), _const_spec(sgb_full.shape),
            _const_spec(w_o_dn.shape), _const_spec(w_o_sg.shape), _const_spec(w_out.shape),
            _const_spec((1, d)), _const_spec((ne, d)), _const_spec((ne, 1)),
        ],
        out_specs=(row(d), row(d // 2), col(2 * TOP_K), col(2 * TOP_K), _const_spec((ne, LANES))),
        scratch_shapes=[pltpu.VMEM((tm, SG_WIDTH), BF16), pltpu.VMEM((ne, LANES), F32),
                        pltpu.VMEM((tm, tm), BF16)],
        compiler_params=_params(("arbitrary",)),
        name="merge_route",
    )(x2, ya, u, v, ga, gb, sg_w.astype(F32), sgb_full, w_o_dn.astype(BF16), w_o_sg.astype(BF16),
      w_out.astype(BF16), norm2_g.reshape(1, d).astype(F32), w_router.T.astype(BF16),
      b_router.reshape(ne, 1).astype(F32))


def _sc_mesh():
    return plsc.VectorSubcoreMesh(core_axis_name="c", subcore_axis_name="s")


def _dispatch(h2, dest3, n_rows):
    n_tok, d = h2.shape
    chunk = dest3.shape[2]
    per_w = n_tok // SC_WORKERS
    n_chunks = per_w // chunk
    nbuf = 2

    @functools.partial(
        pl.kernel, mesh=_sc_mesh(),
        out_type=jax.ShapeDtypeStruct((n_rows, d), h2.dtype),
        scratch_types=[
            pltpu.VMEM((n_chunks, TOP_K, chunk), jnp.int32),
            pltpu.VMEM((nbuf, chunk, d), h2.dtype),
            pltpu.SemaphoreType.DMA((nbuf,)),
            pltpu.SemaphoreType.DMA((nbuf,)),
        ],
        name="moe_dispatch",
    )
    def k(h2_hbm, dest_hbm, xs_hbm, idx_v, rows_v, sem_in, sem_out):
        wid = lax.axis_index("s") * SC_CORES + lax.axis_index("c")
        base = wid * per_w
        pltpu.sync_copy(dest_hbm.at[pl.ds(wid * n_chunks, n_chunks)], idx_v)
        loads = [None] * n_chunks
        stores = [None] * n_chunks
        for j in range(n_chunks + 1):
            if j < n_chunks:
                b = j % nbuf
                if j >= nbuf:
                    for cp in stores[j - nbuf]:
                        cp.wait()
                t0 = pl.multiple_of(base + j * chunk, chunk)
                loads[j] = pltpu.async_copy(h2_hbm.at[pl.ds(t0, chunk)], rows_v.at[b], sem_in.at[b])
            if j >= 1:
                pb = (j - 1) % nbuf
                loads[j - 1].wait()
                stores[j - 1] = [
                    pltpu.async_copy(rows_v.at[pb], xs_hbm.at[idx_v.at[j - 1, kk]], sem_out.at[pb])
                    for kk in range(TOP_K)]
        for j in range(max(n_chunks - nbuf, 0), n_chunks):
            for cp in stores[j]:
                cp.wait()

    return k(h2, dest3)


def _collect(y, dest3):
    n_tok = dest3.shape[0] * dest3.shape[2]
    chunk = dest3.shape[2]
    d = y.shape[1]
    per_w = n_tok // SC_WORKERS
    n_chunks = per_w // chunk
    n_items = n_chunks * TOP_K
    nbuf = 3

    @functools.partial(
        pl.kernel, mesh=_sc_mesh(),
        out_type=jax.ShapeDtypeStruct((TOP_K, n_tok, d), y.dtype),
        scratch_types=[
            pltpu.VMEM((n_chunks, TOP_K, chunk), jnp.int32),
            pltpu.VMEM((nbuf, chunk, d), y.dtype),
            pltpu.SemaphoreType.DMA((nbuf,)),
            pltpu.SemaphoreType.DMA((nbuf,)),
        ],
        name="moe_collect",
    )
    def k(y_hbm, dest_hbm, yg_hbm, idx_v, rows_v, sem_in, sem_out):
        wid = lax.axis_index("s") * SC_CORES + lax.axis_index("c")
        base = wid * per_w
        pltpu.sync_copy(dest_hbm.at[pl.ds(wid * n_chunks, n_chunks)], idx_v)
        gathers = [None] * n_items
        stores = [None] * n_items
        for m in range(n_items + 1):
            if m < n_items:
                b = m % nbuf
                if m >= nbuf:
                    stores[m - nbuf].wait()
                j, kk = divmod(m, TOP_K)
                gathers[m] = pltpu.async_copy(y_hbm.at[idx_v.at[j, kk]], rows_v.at[b], sem_in.at[b])
            if m >= 1:
                pb = (m - 1) % nbuf
                j, kk = divmod(m - 1, TOP_K)
                t0 = pl.multiple_of(base + j * chunk, chunk)
                gathers[m - 1].wait()
                stores[m - 1] = pltpu.async_copy(rows_v.at[pb], yg_hbm.at[kk, pl.ds(t0, chunk)], sem_out.at[pb])
        for m in range(max(n_items - nbuf, 0), n_items):
            stores[m].wait()

    return k(y, dest3)


_BLK_VALID, _BLK_FIRST, _BLK_LAST, _BLK_HALF = 1, 2, 4, 8


def _expert_kernel(be_ref, nxt_ref, flag_ref, xs_ref, wgu_hbm, bgu_ref, wd_hbm, bd_ref, y_ref,
                   wgu_f, wd_f, wgu_s, wd_s, sem):
    i = pl.program_id(0)
    tm = xs_ref.shape[0]
    dff = wd_s.shape[0]
    flags = flag_ref[i]
    valid = (flags & _BLK_VALID) != 0
    has_next = nxt_ref[i] >= 0

    def weight_copies(e):
        return (pltpu.make_async_copy(wgu_hbm.at[e], wgu_f, sem.at[0]),
                pltpu.make_async_copy(wd_hbm.at[e], wd_f, sem.at[1]))

    def round_weights():
        wgu_s[...] = wgu_f[...].astype(BF16)
        wd_s[...] = wd_f[...].astype(BF16)

    @pl.when(i == 0)
    def _():
        for cp in weight_copies(be_ref[0]):
            cp.start()
        for cp in weight_copies(be_ref[0]):
            cp.wait()
        round_weights()

    @pl.when(jnp.logical_and((flags & _BLK_FIRST) != 0, has_next))
    def _():
        for cp in weight_copies(nxt_ref[i]):
            cp.start(priority=1)

    def ffn(rows):
        x = _unpack_bf16_pairs(xs_ref[rows, :]).astype(BF16)
        gu = _dot(x, wgu_s[...]) + bgu_ref[0]
        x_glu = jnp.minimum(gu[:, :dff], SWIGLU_LIMIT)
        x_lin = jnp.clip(gu[:, dff:], -SWIGLU_LIMIT, SWIGLU_LIMIT)
        act = x_glu * _sigmoid(SWIGLU_ALPHA * x_glu) * (x_lin + 1.0)
        y_ref[rows, :] = _pack_bf16_pairs(_dot(act.astype(BF16), wd_s[...]) + bd_ref[0])

    half = (flags & _BLK_HALF) != 0

    @pl.when(jnp.logical_and(valid, jnp.logical_not(half)))
    def _():
        ffn(slice(0, tm))

    @pl.when(jnp.logical_and(valid, half))
    def _():
        ffn(slice(0, tm // 2))
        y_ref[tm // 2:, :] = jnp.zeros((tm - tm // 2, y_ref.shape[1]), y_ref.dtype)

    @pl.when(jnp.logical_not(valid))
    def _():
        y_ref[...] = jnp.zeros_like(y_ref)

    @pl.when(jnp.logical_and((flags & _BLK_LAST) != 0, has_next))
    def _():
        for cp in weight_copies(nxt_ref[i]):
            cp.wait()
        round_weights()


def _expert_ffn(xs, block_expert, block_next, block_flags, w_gate_up, b_gate_up, w_down, b_down):
    n_rows = xs.shape[0]
    tm = MOE_TILE
    ne, d, dff2 = w_gate_up.shape
    dff = dff2 // 2
    n_blocks = n_rows // tm
    grid_spec = pltpu.PrefetchScalarGridSpec(
        num_scalar_prefetch=3,
        grid=(n_blocks,),
        in_specs=[
            pl.BlockSpec((tm, d // 2), lambda i, be, nx, fl: (i, 0)),
            pl.BlockSpec(memory_space=pl.ANY),
            pl.BlockSpec((1, 1, dff2), lambda i, be, nx, fl: (be[i], 0, 0)),
            pl.BlockSpec(memory_space=pl.ANY),
            pl.BlockSpec((1, 1, d), lambda i, be, nx, fl: (be[i], 0, 0)),
        ],
        out_specs=pl.BlockSpec((tm, d // 2), lambda i, be, nx, fl: (i, 0)),
        scratch_shapes=[
            pltpu.VMEM((d, dff2), F32), pltpu.VMEM((dff, d), F32),
            pltpu.VMEM((d, dff2), BF16), pltpu.VMEM((dff, d), BF16),
            pltpu.SemaphoreType.DMA((2,)),
        ],
    )
    return pl.pallas_call(
        _expert_kernel,
        out_shape=jax.ShapeDtypeStruct((n_rows, d // 2), jnp.int32),
        grid_spec=grid_spec,
        compiler_params=_params(("arbitrary",)),
        name="expert_ffn",
    )(block_expert, block_next, block_flags, xs, w_gate_up, b_gate_up.reshape(ne, 1, dff2).astype(F32),
      w_down, b_down.reshape(ne, 1, d).astype(F32))


def _final_kernel(x1_ref, yg_ref, rw_ref, gf_ref, *rest):
    o_ref = rest[-1]
    acc = x1_ref[...]
    rw = rw_ref[...]
    for kk in range(TOP_K):
        acc = acc + rw[:, kk:kk + 1] * _unpack_bf16_pairs(yg_ref[kk])
    ms = jnp.mean(acc * acc, axis=-1, keepdims=True)
    o_ref[...] = acc * lax.rsqrt(ms + RMS_EPS) * gf_ref[...]


def _combine_final(x1, yg_part, rw_tok, norm_f_g, part, out_prev):
    t, d = x1.shape
    tm = ROW_TILE
    steps = yg_part.shape[1] // tm
    off = part * steps
    in_specs = [
        pl.BlockSpec((tm, d), lambda i: (i + off, 0)),
        pl.BlockSpec((TOP_K, tm, d // 2), lambda i: (0, i, 0)),
        pl.BlockSpec((tm, 2 * TOP_K), lambda i: (i + off, 0)),
        _const_spec((1, d)),
    ]
    args = [x1, yg_part, rw_tok, norm_f_g.reshape(1, d).astype(F32)]
    aliases = {}
    if out_prev is not None:
        in_specs.append(pl.BlockSpec(memory_space=pl.ANY))
        args.append(out_prev)
        aliases = {len(args) - 1: 0}
    return pl.pallas_call(
        _final_kernel,
        out_shape=jax.ShapeDtypeStruct((t, d), F32),
        grid=(steps,),
        in_specs=in_specs,
        out_specs=pl.BlockSpec((tm, d), lambda i: (i + off, 0)),
        input_output_aliases=aliases,
        compiler_params=_params(("parallel",)),
        name="combine_final",
    )(*args)


def _routing_tables(route, counts, n_tok):
    tm = MOE_TILE
    n_blocks = (n_tok * TOP_K) // tm + N_EXPERTS
    cnt = counts[:, 0].astype(jnp.int32)
    padded = (cnt + tm - 1) // tm * tm
    pad_end = jnp.cumsum(padded)
    pad_start = pad_end - padded
    eid = route[:TOP_K]
    rank = route[TOP_K:]
    start = jnp.sum(jnp.where(eid[..., None] == jnp.arange(N_EXPERTS, dtype=jnp.int32),
                              pad_start, 0), axis=-1)
    dest = (start + rank).astype(jnp.int32)
    dest3 = dest.reshape(TOP_K, n_tok // SC_CHUNK, SC_CHUNK).transpose(1, 0, 2)
    n_valid = (pad_end[-1] // tm).astype(jnp.int32)
    blk = jnp.arange(n_blocks, dtype=jnp.int32)
    valid = blk < n_valid
    first_row = jnp.minimum(blk, n_valid - 1) * tm
    ids = jnp.arange(N_EXPERTS, dtype=jnp.int32)
    fr = first_row[:, None]
    owner = (pad_start[None, :] <= fr) & (fr < pad_end[None, :])
    pick = lambda tbl: jnp.sum(jnp.where(owner, tbl[None, :], 0), axis=1).astype(jnp.int32)
    be = pick(ids)
    later = (ids[None, :] > ids[:, None]) & (padded[None, :] > 0)
    next_e = jnp.min(jnp.where(later, ids[None, :], N_EXPERTS), axis=1)
    nxt = pick(jnp.where(next_e == N_EXPERTS, -1, next_e))
    start_b = pick(pad_start)
    is_first = valid & (first_row == start_b)
    is_last = valid & (first_row + tm == pick(pad_end))
    rows_used = jnp.clip(pick(cnt) - (first_row - start_b), 0, tm)
    is_half = valid & (rows_used <= tm // 2)
    flags = (valid * _BLK_VALID + is_first * _BLK_FIRST + is_last * _BLK_LAST
             + is_half * _BLK_HALF).astype(jnp.int32)
    return dest3, be, nxt, flags, n_blocks * tm


def _layer(x, norm1_g, w_in, conv_w, dn_a_log, dn_dt_bias, dn_norm_g, w_o_dn, sg_ln_g, sg_ln_b,
           sg_w, sg_b, w_o_sg, w_out, norm2_g, w_router, b_router, w_gate_up, b_gate_up, w_down, b_down,
           norm_f_g):
    bsz, seq, d = x.shape
    t = bsz * seq
    x2 = x.reshape(t, d)
    qkv, zs, bg, bgt, u, v, ga, gb = _project(x2, seq, norm1_g, w_in, conv_w, dn_a_log, dn_dt_bias,
                                              sg_ln_g, sg_ln_b)
    ya = _gated_delta(qkv, bg, bgt, zs, dn_norm_g, bsz, seq)
    x1, h2, route, rw, counts = _merge_route(x2, ya, u, v, ga, gb, sg_w, sg_b, w_o_dn, w_o_sg, w_out,
                                             norm2_g, w_router, b_router)
    dest3, block_expert, block_next, block_flags, n_rows = _routing_tables(route, counts, t)
    xs = _dispatch(h2, dest3, n_rows)
    y = _expert_ffn(xs, block_expert, block_next, block_flags, w_gate_up, b_gate_up, w_down, b_down)
    rw_tok = rw.T
    chunks_per_part = dest3.shape[0] // COMBINE_PARTS
    out = None
    for part in range(COMBINE_PARTS):
        yg = _collect(y, dest3[part * chunks_per_part:(part + 1) * chunks_per_part])
        out = _combine_final(x1, yg, rw_tok, norm_f_g, part, out)
    return out.reshape(bsz, seq, d)


def kernel(x, norm1_g, w_in, conv_w, dn_a_log, dn_dt_bias, dn_norm_g, w_o_dn, sg_ln_g, sg_ln_b, sg_w, sg_b, w_o_sg, w_out, norm2_g, w_router, b_router, w_gate_up, b_gate_up, w_down, b_down, norm_f_g):
    assert norm1_g.shape[0] == 1, "stacked layers are not supported"
    take = lambda a: a[0]
    return _layer(
        x, take(norm1_g), take(w_in), take(conv_w), take(dn_a_log), take(dn_dt_bias), take(dn_norm_g),
        take(w_o_dn), take(sg_ln_g), take(sg_ln_b), take(sg_w), take(sg_b), take(w_o_sg), take(w_out),
        take(norm2_g), take(w_router), take(b_router), take(w_gate_up), take(b_gate_up), take(w_down),
        take(b_down), norm_f_g)
```
